```python
import jax, jax.numpy as jnp
from jax import lax
import numpy as np

D_MODEL = 1024
BATCH = 8
SEQ = 4096
DEPTH = 1

POOL_GROUPS = 4
POOL_GROUP_DIM = 128
POOL_DIM = POOL_GROUPS * POOL_GROUP_DIM
POOL_WINDOWS = (2, 4, 8, 16)
DN_HEADS = 8
DN_HEAD_DIM = 128
DN_DIM = DN_HEADS * DN_HEAD_DIM
SHORT_CONV = 5
CHUNK = 64
FFN_DIM = 2816
FFN_CONV = 3
NORM_EPS = 1e-6
L2_EPS = 1e-6

IN_SIZES = (POOL_DIM, DN_DIM, DN_DIM, DN_DIM, DN_DIM,
            DN_HEADS, DN_HEADS, DN_HEADS, DN_HEADS, D_MODEL, D_MODEL)
IN_DIM = sum(IN_SIZES)
SPLIT_POINTS = tuple(int(s) for s in np.cumsum(IN_SIZES)[:-1])

kernel_name = "hybrid_pool_deltanet_convffn_encoder"


def rmsnorm(x, w):
    xf = x.astype(jnp.float32)
    y = xf * lax.rsqrt(jnp.mean(xf * xf, axis=-1, keepdims=True) + NORM_EPS)
    return (y * w.astype(jnp.float32)).astype(x.dtype)


def dwconv_centered(x, w, b=None):
    K = w.shape[0]
    S = x.shape[1]
    pad = K // 2
    xp = jnp.pad(x, ((0, 0), (pad, pad), (0, 0)))
    y = w[0] * xp[:, 0:S]
    for t in range(1, K):
        y = y + w[t] * xp[:, t:t + S]
    if b is not None:
        y = y + b
    return y


def multiscale_pool(p):
    B, S = p.shape[0], p.shape[1]
    pf = p.astype(jnp.float32).reshape(B, S, POOL_GROUPS, POOL_GROUP_DIM)
    c = jnp.concatenate([jnp.zeros((B, 1, POOL_GROUPS, POOL_GROUP_DIM), jnp.float32),
                         jnp.cumsum(pf, axis=1)], axis=1)
    pos = jnp.arange(S)
    outs = []
    for g, w in enumerate(POOL_WINDOWS):
        start = jnp.clip(pos - w // 2, 0, S)
        end = jnp.clip(pos + w - w // 2, 0, S)
        cg = c[:, :, g]
        wsum = jnp.take(cg, end, axis=1) - jnp.take(cg, start, axis=1)
        cnt = (end - start).astype(jnp.float32)[None, :, None]
        outs.append(wsum / cnt - pf[:, :, g])
    return jnp.stack(outs, axis=2)


def l2norm(t):
    return t * lax.rsqrt(jnp.sum(t * t, axis=-1, keepdims=True) + L2_EPS)


def gated_delta_rule(q, k, v, g, beta):
    B, S, H, dk = q.shape
    dv = v.shape[-1]
    N = S // CHUNK

    def chunks(t):
        t = jnp.swapaxes(t, 1, 2)
        return t.reshape((B, H, N, CHUNK) + t.shape[3:])

    q, k, v, g, beta = chunks(q), chunks(k), chunks(v), chunks(g), chunks(beta)
    G = jnp.cumsum(g, axis=-1)
    idx = jnp.arange(CHUNK)
    incl = idx[:, None] >= idx[None, :]
    strict = idx[:, None] > idx[None, :]
    diff = G[..., :, None] - G[..., None, :]
    decay = jnp.where(incl, jnp.exp(jnp.minimum(diff, 0.0)), 0.0)
    kb = k * beta[..., None]
    vb = v * beta[..., None]
    L = jnp.where(strict, jnp.einsum('bhnid,bhnjd->bhnij', kb, k) * decay, 0.0)
    eye = jnp.eye(CHUNK, dtype=L.dtype)
    rhs = jnp.concatenate([kb * jnp.exp(G)[..., None], vb], axis=-1)
    sol = lax.linalg.triangular_solve(L + eye, rhs, left_side=True, lower=True,
                                      unit_diagonal=True)
    w_c, u_c = sol[..., :dk], sol[..., dk:]
    attn = jnp.einsum('bhnid,bhnjd->bhnij', q, k) * decay
    qg = q * jnp.exp(G)[..., None]
    G_last = G[..., -1:]
    kd = k * jnp.exp(G_last - G)[..., None]
    gl = jnp.exp(G_last[..., 0])
    xs = (jnp.moveaxis(qg, 2, 0), jnp.moveaxis(attn, 2, 0), jnp.moveaxis(w_c, 2, 0),
          jnp.moveaxis(u_c, 2, 0), jnp.moveaxis(kd, 2, 0), jnp.moveaxis(gl, 2, 0))

    def step(state, inp):
        qg_n, attn_n, w_n, u_n, kd_n, gl_n = inp
        v_new = u_n - jnp.einsum('bhcd,bhde->bhce', w_n, state)
        o = (jnp.einsum('bhcd,bhde->bhce', qg_n, state)
             + jnp.einsum('bhij,bhje->bhie', attn_n, v_new))
        state = state * gl_n[..., None, None] + jnp.einsum('bhcd,bhce->bhde', kd_n, v_new)
        return state, o

    s0 = jnp.zeros((B, H, dk, dv), jnp.float32)
    _, o = lax.scan(step, s0, xs)
    o = jnp.moveaxis(o, 0, 2).reshape(B, H, S, dv)
    return jnp.swapaxes(o, 1, 2)


def setup_inputs(seed: int = 0) -> dict:
    key = jax.random.key(seed)
    ks = jax.random.split(key, 20)
    f32 = jnp.float32
    nrm = lambda k, shape, scale: jax.random.normal(k, shape, f32) * scale
    dt = jnp.exp(jax.random.uniform(ks[7], (DEPTH, 2, DN_HEADS), f32,
                                    minval=float(np.log(1e-3)), maxval=float(np.log(1e-1))))
    return {
        "x": nrm(ks[0], (BATCH, SEQ, D_MODEL), 1.0),
        "norm1_w": 1.0 + nrm(ks[1], (DEPTH, D_MODEL), 0.02),
        "w_in": nrm(ks[2], (DEPTH, D_MODEL, IN_DIM), D_MODEL ** -0.5),
        "pool_w": nrm(ks[3], (DEPTH, POOL_GROUPS, POOL_GROUP_DIM, POOL_GROUP_DIM), POOL_GROUP_DIM ** -0.5),
        "pool_scale": 1.0 + nrm(ks[4], (DEPTH, POOL_DIM), 0.02),
        "pool_out": nrm(ks[5], (DEPTH, POOL_DIM, D_MODEL), POOL_DIM ** -0.5),
        "qkv_conv_w": nrm(ks[6], (DEPTH, SHORT_CONV, 3 * DN_DIM), SHORT_CONV ** -0.5),
        "a_log": jnp.log(jax.random.uniform(ks[8], (DEPTH, 2, DN_HEADS), f32, minval=1.0, maxval=16.0)),
        "dt_bias": dt + jnp.log(-jnp.expm1(-dt)),
        "dn_norm_w": 1.0 + nrm(ks[9], (DEPTH, DN_HEAD_DIM), 0.02),
        "dn_out": nrm(ks[10], (DEPTH, DN_DIM, D_MODEL), DN_DIM ** -0.5),
        "w_o": nrm(ks[11], (DEPTH, D_MODEL, D_MODEL), D_MODEL ** -0.5),
        "norm2_w": 1.0 + nrm(ks[12], (DEPTH, D_MODEL), 0.02),
        "ffn_up": nrm(ks[13], (DEPTH, D_MODEL, 2 * FFN_DIM), D_MODEL ** -0.5),
        "ffn_conv_w": nrm(ks[14], (DEPTH, FFN_CONV, 2 * FFN_DIM), FFN_CONV ** -0.5),
        "ffn_conv_b": nrm(ks[15], (DEPTH, 2 * FFN_DIM), 0.02),
        "ffn_down": nrm(ks[16], (DEPTH, FFN_DIM, D_MODEL), FFN_DIM ** -0.5),
        "final_norm_w": 1.0 + nrm(ks[17], (D_MODEL,), 0.02),
    }


def reference(x, norm1_w, w_in, pool_w, pool_scale, pool_out, qkv_conv_w, a_log, dt_bias,
              dn_norm_w, dn_out, w_o, norm2_w, ffn_up, ffn_conv_w, ffn_conv_b, ffn_down,
              final_norm_w):
    B, S, _ = x.shape
    dt_x = x.dtype
    f32 = jnp.float32
    for l in range(DEPTH):
        h = rmsnorm(x, norm1_w[l])
        proj = h @ w_in[l]
        p, q, k, v, z, bf, bb, af, ab, g_pool, g_dn = jnp.split(proj, SPLIT_POINTS, axis=-1)

        pm = multiscale_pool(p)
        pm = jnp.einsum('bsgc,gcd->bsgd', pm, pool_w[l].astype(f32)).reshape(B, S, POOL_DIM)
        pm = (pm * pool_scale[l].astype(f32)).astype(dt_x)
        y_pool = pm @ pool_out[l]

        qkv = jax.nn.silu(dwconv_centered(jnp.concatenate([q, k, v], axis=-1), qkv_conv_w[l]))
        qc, kc, vc = jnp.split(qkv.astype(f32), 3, axis=-1)
        qc = l2norm(qc.reshape(B, S, DN_HEADS, DN_HEAD_DIM)) * (DN_HEAD_DIM ** -0.5)
        kc = l2norm(kc.reshape(B, S, DN_HEADS, DN_HEAD_DIM))
        vc = vc.reshape(B, S, DN_HEADS, DN_HEAD_DIM)
        a_l = a_log[l].astype(f32)
        dtb = dt_bias[l].astype(f32)
        beta_f = jax.nn.sigmoid(bf.astype(f32))
        beta_b = jax.nn.sigmoid(bb.astype(f32))
        g_f = -jnp.exp(a_l[0]) * jax.nn.softplus(af.astype(f32) + dtb[0])
        g_b = -jnp.exp(a_l[1]) * jax.nn.softplus(ab.astype(f32) + dtb[1])
        o_f = gated_delta_rule(qc, kc, vc, g_f, beta_f)
        flip = lambda t: jnp.flip(t, axis=1)
        o_b = flip(gated_delta_rule(flip(qc), flip(kc), flip(vc), flip(g_b), flip(beta_b)))
        o = o_f + o_b
        o = o * lax.rsqrt(jnp.mean(o * o, axis=-1, keepdims=True) + NORM_EPS)
        o = o * dn_norm_w[l].astype(f32) * jax.nn.silu(z.astype(f32).reshape(B, S, DN_HEADS, DN_HEAD_DIM))
        y_dn = o.reshape(B, S, DN_DIM).astype(dt_x) @ dn_out[l]

        merged = jax.nn.sigmoid(g_pool) * y_pool + jax.nn.sigmoid(g_dn) * y_dn
        x = x + merged @ w_o[l]

        h2 = rmsnorm(x, norm2_w[l])
        u = dwconv_centered(h2 @ ffn_up[l], ffn_conv_w[l], ffn_conv_b[l])
        gate, val = jnp.split(u, 2, axis=-1)
        x = x + (jax.nn.silu(gate) * val) @ ffn_down[l]
    return rmsnorm(x, final_norm_w)
```

```python
import functools

import jax
import jax.numpy as jnp
from jax import lax
from jax.experimental import pallas as pl
from jax.experimental.pallas import tpu as pltpu

F32 = jnp.float32
BF16 = jnp.bfloat16

NORM_EPS = 1e-6
L2_EPS = 1e-6
POOL_WINDOWS = (2, 4, 8, 16)
POOL_GROUP_DIM = 128
HEAD_DIM = 128
SHORT_CONV = 5
FFN_CONV = 3

SUBLANES = 8
LANES = 128
HALO = SUBLANES
CHUNK = 128
VMEM_LIMIT = 56 * 1024 * 1024

IN_TILE = 512
MERGE_TILE = 1024
FFN_TILE = 512
FFN_CHUNK = 256


def _shift_rows(a, k):
    n = a.shape[0]
    s = (-k) % n
    return a if s == 0 else pltpu.roll(a, s, axis=0)


def _dot(a, b):
    return jnp.dot(a, b, preferred_element_type=F32)


def _rms(xv, w):
    ms = jnp.mean(xv * xv, axis=-1, keepdims=True)
    return xv * lax.rsqrt(ms + NORM_EPS) * w


def _halo_valid(n_rows, tile, j, nj):
    row = lax.broadcasted_iota(jnp.int32, (n_rows, 1), 0)
    top_ok = jnp.logical_or(row >= HALO, j > 0)
    bot_ok = jnp.logical_or(row < tile + HALO, j < nj - 1)
    return jnp.logical_and(top_ok, bot_ok)


def _halo_specs(tile, width, seq):
    per = tile // HALO
    last = seq // HALO - 1
    prev = pl.BlockSpec((1, HALO, width), lambda b, j: (b, jnp.maximum(j * per - 1, 0), 0))
    main = pl.BlockSpec((1, tile, width), lambda b, j: (b, j, 0))
    nxt = pl.BlockSpec((1, HALO, width), lambda b, j: (b, jnp.minimum((j + 1) * per, last), 0))
    return prev, main, nxt


def _const_spec(shape):
    nd = len(shape)
    return pl.BlockSpec(shape, lambda *_: (0,) * nd, pipeline_mode=pl.Buffered(1))


def _in_proj_kernel(xp_ref, xm_ref, xn_ref, n1w_ref, wq_ref, wk_ref, wv_ref, wp_ref, wz_ref,
                    wg_ref, wgt_ref, cw_ref, poolw_ref, pools_ref,
                    q_out, k_out, v_out, pm_out, z_out, g_out, gt_out, *, tile, seq, heads):
    j = pl.program_id(1)
    nj = pl.num_programs(1)
    n = tile + 2 * HALO
    dn = heads * HEAD_DIM
    xe = jnp.concatenate([xp_ref[0], xm_ref[0], xn_ref[0]], axis=0)
    h = _rms(xe, n1w_ref[...])
    h = jnp.where(_halo_valid(n, tile, j, nj), h, 0.0)
    hb = h.astype(BF16)
    hm = h[HALO:HALO + tile].astype(BF16)

    def conv_silu(w_ref, col0):
        t = _dot(hb, w_ref[...])
        acc = None
        for tt in range(SHORT_CONV):
            sh = _shift_rows(t, tt - SHORT_CONV // 2)[HALO:HALO + tile]
            term = cw_ref[tt:tt + 1, col0:col0 + dn] * sh
            acc = term if acc is None else acc + term
        return acc * jax.nn.sigmoid(acc)

    def l2n(y, scale):
        outs = []
        for hh in range(heads):
            th = y[:, hh * HEAD_DIM:(hh + 1) * HEAD_DIM]
            ss = jnp.sum(th * th, axis=-1, keepdims=True)
            outs.append(th * (lax.rsqrt(ss + L2_EPS) * scale))
        return jnp.concatenate(outs, axis=-1)

    q_out[0] = l2n(conv_silu(wq_ref, 0), HEAD_DIM ** -0.5).astype(BF16)
    k_out[0] = l2n(conv_silu(wk_ref, dn), 1.0).astype(BF16)
    v_out[0] = conv_silu(wv_ref, 2 * dn).astype(BF16)

    pe = _dot(hb, wp_ref[...])
    pos = j * tile + lax.broadcasted_iota(jnp.int32, (tile, 1), 0)
    for g, w in enumerate(POOL_WINDOWS):
        zg = pe[:, g * POOL_GROUP_DIM:(g + 1) * POOL_GROUP_DIM]
        f = zg
        span = 1
        while span < w:
            f = f + _shift_rows(f, span)
            span *= 2
        ws = _shift_rows(f, -(w // 2))[HALO:HALO + tile]
        start = jnp.maximum(pos - w // 2, 0)
        end = jnp.minimum(pos + w - w // 2, seq)
        cnt = (end - start).astype(F32)
        pooled = ws / cnt - zg[HALO:HALO + tile]
        pmg = _dot(pooled.astype(BF16), poolw_ref[g])
        pmg = pmg * pools_ref[:, g * POOL_GROUP_DIM:(g + 1) * POOL_GROUP_DIM]
        pm_out[0, :, g * POOL_GROUP_DIM:(g + 1) * POOL_GROUP_DIM] = pmg.astype(BF16)

    z_out[0] = _dot(hm, wz_ref[...]).astype(BF16)
    g_out[0] = _dot(hm, wg_ref[...]).astype(BF16)
    gt_out[0] = _dot(hm, wgt_ref[...])


def _in_proj(x, n1w, wq, wk, wv, wp, wz, wg, wgt, cw, poolw, pools, heads):
    B, S, D = x.shape
    tile = IN_TILE
    dn = heads * HEAD_DIM
    pool_dim = wp.shape[1]
    xprev, xmain, xnext = _halo_specs(tile, D, S)
    out_spec = lambda width: pl.BlockSpec((1, tile, width), lambda b, j: (b, j, 0))
    return pl.pallas_call(
        functools.partial(_in_proj_kernel, tile=tile, seq=S, heads=heads),
        grid=(B, S // tile),
        in_specs=[xprev, xmain, xnext, _const_spec(n1w.shape), _const_spec(wq.shape),
                  _const_spec(wk.shape), _const_spec(wv.shape), _const_spec(wp.shape),
                  _const_spec(wz.shape), _const_spec(wg.shape), _const_spec(wgt.shape),
                  _const_spec(cw.shape), _const_spec(poolw.shape), _const_spec(pools.shape)],
        out_specs=[out_spec(dn), out_spec(dn), out_spec(dn), out_spec(pool_dim), out_spec(dn),
                   out_spec(wg.shape[1]), out_spec(LANES)],
        out_shape=[jax.ShapeDtypeStruct((B, S, dn), BF16)] * 3
        + [jax.ShapeDtypeStruct((B, S, pool_dim), BF16), jax.ShapeDtypeStruct((B, S, dn), BF16),
           jax.ShapeDtypeStruct((B, S, wg.shape[1]), BF16), jax.ShapeDtypeStruct((B, S, LANES), F32)],
        compiler_params=pltpu.CompilerParams(dimension_semantics=("parallel", "parallel"),
                                             vmem_limit_bytes=VMEM_LIMIT),
        name="in_proj",
    )(x, x, x, n1w, wq, wk, wv, wp, wz, wg, wgt, cw, poolw, pools)


def _softplus(t):
    return jnp.maximum(t, 0.0) + jnp.log1p(jnp.exp(-jnp.abs(t)))


def _scan_sum(a, axis, reverse):
    n = a.shape[axis]
    idx = lax.broadcasted_iota(jnp.int32, a.shape, axis)
    s = 1
    while s < n:
        if reverse:
            a = a + jnp.where(idx < n - s, pltpu.roll(a, n - s, axis=axis), 0.0)
        else:
            a = a + jnp.where(idx >= s, pltpu.roll(a, s, axis=axis), 0.0)
        s *= 2
    return a


def _delta_kernel(alog_ref, dtb_ref, q_ref, k_ref, v_ref, z_ref, gates_ref, nw_ref, o_ref,
                  gcol_s, bcol_s, grow_s, wq_s, ak_s, u_s, gl_s, oacc_s, *, nchunks):
    C = CHUNK
    N = nchunks
    hh = pl.program_id(1)

    pad = jnp.zeros((LANES - N, C), F32)
    for d in range(2):
        a = jnp.exp(jnp.full((1, 1), alog_ref[d, hh], F32))
        g_r = -a * _softplus(gates_ref[0, 2 + d, 0] + dtb_ref[d, hh])
        G_r = _scan_sum(g_r, 1, d == 1)
        grow_s[d] = G_r
        b_r = jax.nn.sigmoid(gates_ref[0, d, 0])
        G_c = jnp.concatenate([G_r, pad], axis=0).T
        b_c = jnp.concatenate([b_r, pad], axis=0).T
        for n in range(N):
            gcol_s[d, n] = jnp.broadcast_to(G_c[:, n:n + 1], (C, LANES))
            bcol_s[d, n] = jnp.broadcast_to(b_c[:, n:n + 1], (C, LANES))

    ii = lax.broadcasted_iota(jnp.int32, (C, C), 0)
    jj = lax.broadcasted_iota(jnp.int32, (C, C), 1)

    def prep(n, carry):
        r0 = pl.multiple_of(n * C, C)
        k = k_ref[0, pl.ds(r0, C), :]
        q = q_ref[0, pl.ds(r0, C), :]
        kf = k.astype(F32)
        qf = q.astype(F32)
        vf = v_ref[0, pl.ds(r0, C), :].astype(F32)
        kkqk = lax.dot_general(jnp.concatenate([k, q], axis=0), k, (((1,), (1,)), ((), ())),
                               preferred_element_type=F32)
        KK = kkqk[:C]
        QK = kkqk[C:]
        for d in range(2):
            GC = gcol_s[d, n]
            BC = bcol_s[d, n]
            GR = grow_s[d, pl.ds(n, 1), :]
            incl = (ii >= jj) if d == 0 else (ii <= jj)
            strict = (ii > jj) if d == 0 else (ii < jj)
            decay = jnp.where(incl, jnp.exp(jnp.minimum(GC - GR, 0.0)), 0.0)
            L = jnp.where(strict, BC * KK * decay, 0.0)
            attn = QK * decay
            Y = -L
            Lb = L.astype(BF16)
            P = _dot(Lb, Lb)
            rounds = C.bit_length() - 2
            for r in range(rounds):
                Pb = P.astype(BF16)
                Y = Y + P + _dot(Y.astype(BF16), Pb)
                if r < rounds - 1:
                    P = _dot(Pb, Pb)
            eG = jnp.exp(GC)
            rhs = jnp.concatenate([kf * BC * eG, vf * BC], axis=1)
            wu = rhs + _dot(Y.astype(BF16), rhs.astype(BF16))
            g_last = GC[C - 1:C, :] if d == 0 else GC[0:1, :]
            kd = kf * jnp.exp(g_last - GC)
            wq_s[d, n] = jnp.concatenate([wu[:, :HEAD_DIM], qf * eG], axis=0).astype(BF16)
            ak_s[d, n] = jnp.concatenate([attn, kd.T], axis=0).astype(BF16)
            u_s[d, n] = wu[:, HEAD_DIM:]
            gl_s[d, pl.ds(n, 1), :] = jnp.exp(g_last)
        return carry

    lax.fori_loop(0, N, prep, 0)

    oacc_s[...] = jnp.zeros(oacc_s.shape, F32)

    def step(s, states):
        new_states = []
        for d in range(2):
            n = s if d == 0 else N - 1 - s
            S = states[d]
            p1 = _dot(wq_s[d, n], S.astype(BF16))
            v_new = u_s[d, n] - p1[:C]
            p2 = _dot(ak_s[d, n], v_new.astype(BF16))
            r0 = pl.multiple_of(n * C, C)
            oacc_s[pl.ds(r0, C), :] += p1[C:] + p2[:C]
            new_states.append(S * gl_s[d, pl.ds(n, 1), :] + p2[C:])
        return tuple(new_states)

    zero = jnp.zeros((HEAD_DIM, HEAD_DIM), F32)
    lax.fori_loop(0, N, step, (zero, zero))

    def fin(n, carry):
        r0 = pl.multiple_of(n * C, C)
        o = _rms(oacc_s[pl.ds(r0, C), :], nw_ref[...])
        zz = z_ref[0, pl.ds(r0, C), :].astype(F32)
        o_ref[0, pl.ds(r0, C), :] = (o * (zz * jax.nn.sigmoid(zz))).astype(BF16)
        return carry

    lax.fori_loop(0, N, fin, 0)


def _delta(q, k, v, z, gates_rows, a_log, dt_bias, nw):
    B, S, dn = q.shape
    heads = dn // HEAD_DIM
    N = S // CHUNK
    head_spec = pl.BlockSpec((1, S, HEAD_DIM), lambda b, h: (b, 0, h))
    smem = pl.BlockSpec(memory_space=pltpu.SMEM)
    return pl.pallas_call(
        functools.partial(_delta_kernel, nchunks=N),
        grid=(B, heads),
        in_specs=[smem, smem, head_spec, head_spec, head_spec, head_spec,
                  pl.BlockSpec((1, 4, 1, N, CHUNK), lambda b, h: (b, 0, h, 0, 0)),
                  pl.BlockSpec((1, HEAD_DIM), lambda b, h: (0, 0))],
        out_specs=head_spec,
        out_shape=jax.ShapeDtypeStruct((B, S, dn), BF16),
        scratch_shapes=[
            pltpu.VMEM((2, N, CHUNK, LANES), F32),
            pltpu.VMEM((2, N, CHUNK, LANES), F32),
            pltpu.VMEM((2, N, CHUNK), F32),
            pltpu.VMEM((2, N, 2 * CHUNK, HEAD_DIM), BF16),
            pltpu.VMEM((2, N, CHUNK + HEAD_DIM, CHUNK), BF16),
            pltpu.VMEM((2, N, CHUNK, HEAD_DIM), F32),
            pltpu.VMEM((2, N, LANES), F32),
            pltpu.VMEM((S, HEAD_DIM), F32),
        ],
        compiler_params=pltpu.CompilerParams(dimension_semantics=("parallel", "parallel"),
                                             vmem_limit_bytes=VMEM_LIMIT),
        name="delta_rule",
    )(a_log, dt_bias, q, k, v, z, gates_rows, nw)


def _merge_kernel(o_ref, pm_ref, g_ref, x_ref, dn_ref, po_ref, wo_ref, out_ref):
    d = out_ref.shape[-1]
    y_dn = _dot(o_ref[0], dn_ref[...])
    y_pool = _dot(pm_ref[0], po_ref[...])
    g = g_ref[0].astype(F32)
    merged = jax.nn.sigmoid(g[:, :d]) * y_pool + jax.nn.sigmoid(g[:, d:]) * y_dn
    out_ref[0] = x_ref[0] + _dot(merged.astype(BF16), wo_ref[...])


def _merge(o, pm, g, x, dn_out, pool_out, w_o):
    B, S, D = x.shape
    tile = MERGE_TILE
    spec = lambda width: pl.BlockSpec((1, tile, width), lambda b, j: (b, j, 0))
    return pl.pallas_call(
        _merge_kernel,
        grid=(B, S // tile),
        in_specs=[spec(o.shape[-1]), spec(pm.shape[-1]), spec(g.shape[-1]), spec(D),
                  _const_spec(dn_out.shape), _const_spec(pool_out.shape), _const_spec(w_o.shape)],
        out_specs=spec(D),
        out_shape=jax.ShapeDtypeStruct((B, S, D), F32),
        compiler_params=pltpu.CompilerParams(dimension_semantics=("parallel", "parallel"),
                                             vmem_limit_bytes=VMEM_LIMIT),
        name="merge",
    )(o, pm, g, x, dn_out, pool_out, w_o)


def _ffn_kernel(xp_ref, xm_ref, xn_ref, n2w_ref, upg_ref, upv_ref, cwg_ref, cwv_ref, cbg_ref,
                cbv_ref, down_ref, fw_ref, out_ref, act_s, *, tile, nfc):
    j = pl.program_id(1)
    nj = pl.num_programs(1)
    n = tile + 2 * HALO
    xm = xm_ref[0]
    xe = jnp.concatenate([xp_ref[0], xm, xn_ref[0]], axis=0)
    h = _rms(xe, n2w_ref[...])
    hb = jnp.where(_halo_valid(n, tile, j, nj), h, 0.0).astype(BF16)

    def conv(t, cw_ref, cb_ref, c):
        acc = None
        for tt in range(FFN_CONV):
            sh = _shift_rows(t, tt - FFN_CONV // 2)[HALO:HALO + tile]
            term = cw_ref[c, tt:tt + 1, :] * sh
            acc = term if acc is None else acc + term
        return acc + cb_ref[c]

    for c in range(nfc):
        gate = conv(_dot(hb, upg_ref[c]), cwg_ref, cbg_ref, c)
        val = conv(_dot(hb, upv_ref[c]), cwv_ref, cbv_ref, c)
        act_s[:, c * FFN_CHUNK:(c + 1) * FFN_CHUNK] = (gate * jax.nn.sigmoid(gate) * val).astype(BF16)

    x2 = xm + _dot(act_s[...], down_ref[...])
    out_ref[0] = _rms(x2, fw_ref[...])


def _ffn(x, n2w, upg, upv, cwg, cwv, cbg, cbv, down, fw):
    B, S, D = x.shape
    tile = FFN_TILE
    nfc = upg.shape[0]
    xprev, xmain, xnext = _halo_specs(tile, D, S)
    return pl.pallas_call(
        functools.partial(_ffn_kernel, tile=tile, nfc=nfc),
        grid=(B, S // tile),
        in_specs=[xprev, xmain, xnext, _const_spec(n2w.shape), _const_spec(upg.shape),
                  _const_spec(upv.shape), _const_spec(cwg.shape), _const_spec(cwv.shape),
                  _const_spec(cbg.shape), _const_spec(cbv.shape), _const_spec(down.shape),
                  _const_spec(fw.shape)],
        out_specs=pl.BlockSpec((1, tile, D), lambda b, j: (b, j, 0)),
        out_shape=jax.ShapeDtypeStruct((B, S, D), F32),
        scratch_shapes=[pltpu.VMEM((tile, nfc * FFN_CHUNK), BF16)],
        compiler_params=pltpu.CompilerParams(dimension_semantics=("parallel", "parallel"),
                                             vmem_limit_bytes=VMEM_LIMIT),
        name="ffn",
    )(x, x, x, n2w, upg, upv, cwg, cwv, cbg, cbv, down, fw)


def kernel(x, norm1_w, w_in, pool_w, pool_scale, pool_out, qkv_conv_w, a_log, dt_bias, dn_norm_w,
           dn_out, w_o, norm2_w, ffn_up, ffn_conv_w, ffn_conv_b, ffn_down, final_norm_w):
    B, S, D = x.shape
    depth = w_in.shape[0]
    pool_dim = pool_out.shape[1]
    dn = dn_out.shape[1]
    heads = a_log.shape[-1]
    F = ffn_down.shape[1]
    assert depth == 1, "the final rmsnorm is fused into the (single) layer's channel mixer"
    assert dn == heads * HEAD_DIM and CHUNK == HEAD_DIM == LANES
    assert S % CHUNK == 0 and S // CHUNK <= LANES
    assert S % IN_TILE == 0 and S % MERGE_TILE == 0 and S % FFN_TILE == 0 and F % FFN_CHUNK == 0
    nfc = F // FFN_CHUNK
    N = S // CHUNK
    row = lambda t: t.reshape(1, -1).astype(F32)

    for l in range(depth):
        wl = w_in[l].astype(BF16)
        c0 = pool_dim
        wp = wl[:, :c0]
        wq, wk, wv, wz = (wl[:, c0 + i * dn:c0 + (i + 1) * dn] for i in range(4))
        c1 = c0 + 4 * dn
        wgt = jnp.pad(wl[:, c1:c1 + 4 * heads], ((0, 0), (0, LANES - 4 * heads)))
        wg = wl[:, c1 + 4 * heads:]

        q, k, v, pm, z, g, gates = _in_proj(
            x, row(norm1_w[l]), wq, wk, wv, wp, wz, wg, wgt, qkv_conv_w[l].astype(F32),
            pool_w[l].astype(BF16), row(pool_scale[l]), heads)

        gates_rows = gates[:, :, :4 * heads].reshape(B, N, CHUNK, 4, heads).transpose(0, 3, 4, 1, 2)
        o = _delta(q, k, v, z, gates_rows, a_log[l].astype(F32), dt_bias[l].astype(F32),
                   row(dn_norm_w[l]))

        x = _merge(o, pm, g, x, dn_out[l].astype(BF16), pool_out[l].astype(BF16), w_o[l].astype(BF16))

        up = ffn_up[l].astype(BF16)
        chunked = lambda t: t.reshape(t.shape[0], nfc, FFN_CHUNK).transpose(1, 0, 2)
        cw = ffn_conv_w[l].astype(F32)
        cb = ffn_conv_b[l].astype(F32).reshape(1, -1)
        x = _ffn(x, row(norm2_w[l]), chunked(up[:, :F]), chunked(up[:, F:]), chunked(cw[:, :F]),
                 chunked(cw[:, F:]), chunked(cb[:, :F]), chunked(cb[:, F:]), ffn_down[l].astype(BF16),
                 row(final_norm_w))
    return x
```

```python
import functools

import jax
import jax.numpy as jnp
from jax import lax
from jax.experimental import pallas as pl
from jax.experimental.pallas import tpu as pltpu

F32 = jnp.float32
BF16 = jnp.bfloat16

NORM_EPS = 1e-6
L2_EPS = 1e-6
POOL_WINDOWS = (2, 4, 8, 16)
POOL_GROUP_DIM = 128
HEAD_DIM = 128
SHORT_CONV = 5
FFN_CONV = 3

SUBLANES = 8
LANES = 128
HALO = SUBLANES
CHUNK = 128
DELTA_BATCH = 4
VMEM_LIMIT = 56 * 1024 * 1024

IN_TILE = 512
MERGE_TILE = 1024
FFN_TILE = 512
FFN_CHUNK = 256


def _shift_rows(a, k):
    n = a.shape[0]
    s = (-k) % n
    return a if s == 0 else pltpu.roll(a, s, axis=0)


def _dot(a, b):
    return jnp.dot(a, b, preferred_element_type=F32)


def _rms(xv, w):
    ms = jnp.mean(xv * xv, axis=-1, keepdims=True)
    return xv * lax.rsqrt(ms + NORM_EPS) * w


def _halo_valid(n_rows, tile, j, nj):
    row = lax.broadcasted_iota(jnp.int32, (n_rows, 1), 0)
    top_ok = jnp.logical_or(row >= HALO, j > 0)
    bot_ok = jnp.logical_or(row < tile + HALO, j < nj - 1)
    return jnp.logical_and(top_ok, bot_ok)


def _halo_specs(tile, width, seq):
    per = tile // HALO
    last = seq // HALO - 1
    prev = pl.BlockSpec((1, HALO, width), lambda b, j: (b, jnp.maximum(j * per - 1, 0), 0))
    main = pl.BlockSpec((1, tile, width), lambda b, j: (b, j, 0))
    nxt = pl.BlockSpec((1, HALO, width), lambda b, j: (b, jnp.minimum((j + 1) * per, last), 0))
    return prev, main, nxt


def _const_spec(shape):
    nd = len(shape)
    return pl.BlockSpec(shape, lambda *_: (0,) * nd, pipeline_mode=pl.Buffered(1))


def _in_proj_kernel(xp_ref, xm_ref, xn_ref, n1w_ref, wq_ref, wk_ref, wv_ref, wp_ref, wz_ref,
                    wg_ref, wgt_ref, cw_ref, poolw_ref, pools_ref,
                    q_out, k_out, v_out, pm_out, z_out, g_out, gt_out, *, tile, seq, heads):
    j = pl.program_id(1)
    nj = pl.num_programs(1)
    n = tile + 2 * HALO
    dn = heads * HEAD_DIM
    xe = jnp.concatenate([xp_ref[0], xm_ref[0], xn_ref[0]], axis=0)
    h = _rms(xe, n1w_ref[...])
    h = jnp.where(_halo_valid(n, tile, j, nj), h, 0.0)
    hb = h.astype(BF16)
    hm = h[HALO:HALO + tile].astype(BF16)

    def conv_silu(w_ref, col0):
        t = _dot(hb, w_ref[...])
        acc = None
        for tt in range(SHORT_CONV):
            sh = _shift_rows(t, tt - SHORT_CONV // 2)[HALO:HALO + tile]
            term = cw_ref[tt:tt + 1, col0:col0 + dn] * sh
            acc = term if acc is None else acc + term
        return acc * jax.nn.sigmoid(acc)

    def l2n(y, scale):
        outs = []
        for hh in range(heads):
            th = y[:, hh * HEAD_DIM:(hh + 1) * HEAD_DIM]
            ss = jnp.sum(th * th, axis=-1, keepdims=True)
            outs.append(th * (lax.rsqrt(ss + L2_EPS) * scale))
        return jnp.concatenate(outs, axis=-1)

    q_out[0] = l2n(conv_silu(wq_ref, 0), HEAD_DIM ** -0.5).astype(BF16)
    k_out[0] = l2n(conv_silu(wk_ref, dn), 1.0).astype(BF16)
    v_out[0] = conv_silu(wv_ref, 2 * dn).astype(BF16)

    pe = _dot(hb, wp_ref[...])
    pos = j * tile + lax.broadcasted_iota(jnp.int32, (tile, 1), 0)
    for g, w in enumerate(POOL_WINDOWS):
        zg = pe[:, g * POOL_GROUP_DIM:(g + 1) * POOL_GROUP_DIM]
        f = zg
        span = 1
        while span < w:
            f = f + _shift_rows(f, span)
            span *= 2
        ws = _shift_rows(f, -(w // 2))[HALO:HALO + tile]
        start = jnp.maximum(pos - w // 2, 0)
        end = jnp.minimum(pos + w - w // 2, seq)
        cnt = (end - start).astype(F32)
        pooled = ws / cnt - zg[HALO:HALO + tile]
        pmg = _dot(pooled.astype(BF16), poolw_ref[g])
        pmg = pmg * pools_ref[:, g * POOL_GROUP_DIM:(g + 1) * POOL_GROUP_DIM]
        pm_out[0, :, g * POOL_GROUP_DIM:(g + 1) * POOL_GROUP_DIM] = pmg.astype(BF16)

    z_out[0] = _dot(hm, wz_ref[...]).astype(BF16)
    g_out[0] = _dot(hm, wg_ref[...]).astype(BF16)
    gt_out[0] = _dot(hm, wgt_ref[...])


def _in_proj(x, n1w, wq, wk, wv, wp, wz, wg, wgt, cw, poolw, pools, heads):
    B, S, D = x.shape
    tile = IN_TILE
    dn = heads * HEAD_DIM
    pool_dim = wp.shape[1]
    xprev, xmain, xnext = _halo_specs(tile, D, S)
    out_spec = lambda width: pl.BlockSpec((1, tile, width), lambda b, j: (b, j, 0))
    return pl.pallas_call(
        functools.partial(_in_proj_kernel, tile=tile, seq=S, heads=heads),
        grid=(B, S // tile),
        in_specs=[xprev, xmain, xnext, _const_spec(n1w.shape), _const_spec(wq.shape),
                  _const_spec(wk.shape), _const_spec(wv.shape), _const_spec(wp.shape),
                  _const_spec(wz.shape), _const_spec(wg.shape), _const_spec(wgt.shape),
                  _const_spec(cw.shape), _const_spec(poolw.shape), _const_spec(pools.shape)],
        out_specs=[out_spec(dn), out_spec(dn), out_spec(dn), out_spec(pool_dim), out_spec(dn),
                   out_spec(wg.shape[1]), out_spec(LANES)],
        out_shape=[jax.ShapeDtypeStruct((B, S, dn), BF16)] * 3
        + [jax.ShapeDtypeStruct((B, S, pool_dim), BF16), jax.ShapeDtypeStruct((B, S, dn), BF16),
           jax.ShapeDtypeStruct((B, S, wg.shape[1]), BF16), jax.ShapeDtypeStruct((B, S, LANES), F32)],
        compiler_params=pltpu.CompilerParams(dimension_semantics=("parallel", "parallel"),
                                             vmem_limit_bytes=VMEM_LIMIT),
        name="in_proj",
    )(x, x, x, n1w, wq, wk, wv, wp, wz, wg, wgt, cw, poolw, pools)


def _softplus(t):
    return jnp.maximum(t, 0.0) + jnp.log1p(jnp.exp(-jnp.abs(t)))


def _scan_sum(a, axis, reverse):
    n = a.shape[axis]
    idx = lax.broadcasted_iota(jnp.int32, a.shape, axis)
    s = 1
    while s < n:
        if reverse:
            a = a + jnp.where(idx < n - s, pltpu.roll(a, n - s, axis=axis), 0.0)
        else:
            a = a + jnp.where(idx >= s, pltpu.roll(a, s, axis=axis), 0.0)
        s *= 2
    return a


def _bmm(a, b):
    return jnp.einsum("uik,ukj->uij", a, b, preferred_element_type=F32)


def _bmm_nt(a, b):
    return jnp.einsum("uik,ujk->uij", a, b, preferred_element_type=F32)


def _delta_kernel(alog_ref, dtb_ref, q_ref, k_ref, v_ref, z_ref, gates_ref, nwc_ref, o_ref,
                  gcol_s, grow_s, brow_s, rhs1_s, rhs2_s, ut_s, gl_s, oacc_s, *, nchunks, batch):
    C = CHUNK
    N = nchunks
    U = batch
    D = HEAD_DIM
    hh = pl.program_id(1)

    pad = jnp.zeros((LANES - N, C), F32)
    for d in range(2):
        a = jnp.exp(jnp.full((1, 1), alog_ref[d, hh], F32))
        g_r = -a * _softplus(gates_ref[0, 2 + d, 0] + dtb_ref[d, hh])
        G_r = _scan_sum(g_r, 1, d == 1)
        grow_s[d] = G_r.reshape(N, 1, C)
        brow_s[d] = jax.nn.sigmoid(gates_ref[0, d, 0]).reshape(N, 1, C)
        G_c = jnp.concatenate([G_r, pad], axis=0).T
        for n in range(N):
            gcol_s[d, n] = jnp.broadcast_to(G_c[:, n:n + 1], (C, LANES))

    rr = lax.broadcasted_iota(jnp.int32, (U, C, C), 1)
    cc = lax.broadcasted_iota(jnp.int32, (U, C, C), 2)

    def prep(it, carry):
        n0 = pl.multiple_of(it * U, U)
        r0 = pl.multiple_of(it * (U * C), U * C)
        k = k_ref[0, pl.ds(r0, U * C), :].reshape(U, C, D)
        q = q_ref[0, pl.ds(r0, U * C), :].reshape(U, C, D)
        kf = k.astype(F32)
        kT = jnp.swapaxes(kf, 1, 2)
        qT = jnp.swapaxes(q.astype(F32), 1, 2)
        vT = jnp.swapaxes(v_ref[0, pl.ds(r0, U * C), :].reshape(U, C, D).astype(F32), 1, 2)
        kkq = _bmm_nt(k, jnp.concatenate([k, q], axis=1))
        KK = kkq[:, :, :C]
        QKt = kkq[:, :, C:]
        for d in range(2):
            GC = gcol_s[d, pl.ds(n0, U)]
            GR = grow_s[d, pl.ds(n0, U)]
            BR = brow_s[d, pl.ds(n0, U)]
            incl = (cc >= rr) if d == 0 else (cc <= rr)
            strict = (cc > rr) if d == 0 else (cc < rr)
            decay = jnp.where(incl, jnp.exp(jnp.minimum(GR - GC, 0.0)), 0.0)
            Lt = jnp.where(strict, KK * BR * decay, 0.0)
            attn = QKt * decay
            Y = -Lt
            Lb = Lt.astype(BF16)
            P = _bmm(Lb, Lb)
            rounds = C.bit_length() - 2
            for r in range(rounds):
                Pb = P.astype(BF16)
                if r < rounds - 1:
                    py = _bmm(Pb, jnp.concatenate([Y.astype(BF16), Pb], axis=2))
                    Y = Y + P + py[:, :, :C]
                    P = py[:, :, C:]
                else:
                    Y = Y + P + _bmm(Pb, Y.astype(BF16))
            eG = jnp.exp(GR)
            X = jnp.concatenate([kT * (BR * eG), vT * BR], axis=1)
            wu = X + _bmm(X.astype(BF16), Y.astype(BF16))
            g_last = GC[:, C - 1:C, :] if d == 0 else GC[:, 0:1, :]
            kd = kf * jnp.exp(g_last - GC)
            rhs1_s[d, pl.ds(n0, U)] = jnp.concatenate([wu[:, :D], qT * eG], axis=2).astype(BF16)
            rhs2_s[d, pl.ds(n0, U)] = jnp.concatenate([attn, kd], axis=2).astype(BF16)
            ut_s[d, pl.ds(n0, U)] = wu[:, D:]
            gl_s[d, pl.ds(n0, U)] = jnp.exp(g_last)
        return carry

    lax.fori_loop(0, N // U, prep, 0)

    oacc_s[...] = jnp.zeros(oacc_s.shape, F32)

    def step(s, states):
        new_states = []
        for d in range(2):
            n = s if d == 0 else N - 1 - s
            ST = states[d]
            p1 = _dot(ST.astype(BF16), rhs1_s[d, n])
            vn = ut_s[d, n] - p1[:, :C]
            p2 = _dot(vn.astype(BF16), rhs2_s[d, n])
            oacc_s[n] += p1[:, C:] + p2[:, :C]
            new_states.append(ST * gl_s[d, n] + p2[:, C:])
        return tuple(new_states)

    zero = jnp.zeros((D, D), F32)
    lax.fori_loop(0, N, step, (zero, zero))

    def fin(n, carry):
        r0 = pl.multiple_of(n * C, C)
        ot = oacc_s[n]
        ms = jnp.mean(ot * ot, axis=0, keepdims=True)
        o = (ot * lax.rsqrt(ms + NORM_EPS) * nwc_ref[...]).T
        zz = z_ref[0, pl.ds(r0, C), :].astype(F32)
        o_ref[0, pl.ds(r0, C), :] = (o * (zz * jax.nn.sigmoid(zz))).astype(BF16)
        return carry

    lax.fori_loop(0, N, fin, 0, unroll=4)


def _delta(q, k, v, z, gates_rows, a_log, dt_bias, nw_cols):
    B, S, dn = q.shape
    heads = dn // HEAD_DIM
    N = S // CHUNK
    head_spec = pl.BlockSpec((1, S, HEAD_DIM), lambda b, h: (b, 0, h))
    smem = pl.BlockSpec(memory_space=pltpu.SMEM)
    return pl.pallas_call(
        functools.partial(_delta_kernel, nchunks=N, batch=DELTA_BATCH),
        grid=(B, heads),
        in_specs=[smem, smem, head_spec, head_spec, head_spec, head_spec,
                  pl.BlockSpec((1, 4, 1, N, CHUNK), lambda b, h: (b, 0, h, 0, 0)),
                  pl.BlockSpec((HEAD_DIM, LANES), lambda b, h: (0, 0))],
        out_specs=head_spec,
        out_shape=jax.ShapeDtypeStruct((B, S, dn), BF16),
        scratch_shapes=[
            pltpu.VMEM((2, N, CHUNK, LANES), F32),
            pltpu.VMEM((2, N, 1, CHUNK), F32),
            pltpu.VMEM((2, N, 1, CHUNK), F32),
            pltpu.VMEM((2, N, HEAD_DIM, 2 * CHUNK), BF16),
            pltpu.VMEM((2, N, CHUNK, CHUNK + HEAD_DIM), BF16),
            pltpu.VMEM((2, N, HEAD_DIM, CHUNK), F32),
            pltpu.VMEM((2, N, 1, LANES), F32),
            pltpu.VMEM((N, HEAD_DIM, CHUNK), F32),
        ],
        compiler_params=pltpu.CompilerParams(dimension_semantics=("parallel", "parallel"),
                                             vmem_limit_bytes=VMEM_LIMIT),
        name="delta_rule",
    )(a_log, dt_bias, q, k, v, z, gates_rows, nw_cols)


def _merge_kernel(o_ref, pm_ref, g_ref, x_ref, dn_ref, po_ref, wo_ref, out_ref):
    d = out_ref.shape[-1]
    y_dn = _dot(o_ref[0], dn_ref[...])
    y_pool = _dot(pm_ref[0], po_ref[...])
    g = g_ref[0].astype(F32)
    merged = jax.nn.sigmoid(g[:, :d]) * y_pool + jax.nn.sigmoid(g[:, d:]) * y_dn
    out_ref[0] = x_ref[0] + _dot(merged.astype(BF16), wo_ref[...])


def _merge(o, pm, g, x, dn_out, pool_out, w_o):
    B, S, D = x.shape
    tile = MERGE_TILE
    spec = lambda width: pl.BlockSpec((1, tile, width), lambda b, j: (b, j, 0))
    return pl.pallas_call(
        _merge_kernel,
        grid=(B, S // tile),
        in_specs=[spec(o.shape[-1]), spec(pm.shape[-1]), spec(g.shape[-1]), spec(D),
                  _const_spec(dn_out.shape), _const_spec(pool_out.shape), _const_spec(w_o.shape)],
        out_specs=spec(D),
        out_shape=jax.ShapeDtypeStruct((B, S, D), F32),
        compiler_params=pltpu.CompilerParams(dimension_semantics=("parallel", "parallel"),
                                             vmem_limit_bytes=VMEM_LIMIT),
        name="merge",
    )(o, pm, g, x, dn_out, pool_out, w_o)


def _ffn_kernel(xp_ref, xm_ref, xn_ref, n2w_ref, upg_ref, upv_ref, cwg_ref, cwv_ref, cbg_ref,
                cbv_ref, down_ref, fw_ref, out_ref, act_s, *, tile, nfc):
    j = pl.program_id(1)
    nj = pl.num_programs(1)
    n = tile + 2 * HALO
    xm = xm_ref[0]
    xe = jnp.concatenate([xp_ref[0], xm, xn_ref[0]], axis=0)
    h = _rms(xe, n2w_ref[...])
    hb = jnp.where(_halo_valid(n, tile, j, nj), h, 0.0).astype(BF16)

    def conv(t, cw_ref, cb_ref, c):
        acc = None
        for tt in range(FFN_CONV):
            sh = _shift_rows(t, tt - FFN_CONV // 2)[HALO:HALO + tile]
            term = cw_ref[c, tt:tt + 1, :] * sh
            acc = term if acc is None else acc + term
        return acc + cb_ref[c]

    for c in range(nfc):
        gate = conv(_dot(hb, upg_ref[c]), cwg_ref, cbg_ref, c)
        val = conv(_dot(hb, upv_ref[c]), cwv_ref, cbv_ref, c)
        act_s[:, c * FFN_CHUNK:(c + 1) * FFN_CHUNK] = (gate * jax.nn.sigmoid(gate) * val).astype(BF16)

    x2 = xm + _dot(act_s[...], down_ref[...])
    out_ref[0] = _rms(x2, fw_ref[...])


def _ffn(x, n2w, upg, upv, cwg, cwv, cbg, cbv, down, fw):
    B, S, D = x.shape
    tile = FFN_TILE
    nfc = upg.shape[0]
    xprev, xmain, xnext = _halo_specs(tile, D, S)
    return pl.pallas_call(
        functools.partial(_ffn_kernel, tile=tile, nfc=nfc),
        grid=(B, S // tile),
        in_specs=[xprev, xmain, xnext, _const_spec(n2w.shape), _const_spec(upg.shape),
                  _const_spec(upv.shape), _const_spec(cwg.shape), _const_spec(cwv.shape),
                  _const_spec(cbg.shape), _const_spec(cbv.shape), _const_spec(down.shape),
                  _const_spec(fw.shape)],
        out_specs=pl.BlockSpec((1, tile, D), lambda b, j: (b, j, 0)),
        out_shape=jax.ShapeDtypeStruct((B, S, D), F32),
        scratch_shapes=[pltpu.VMEM((tile, nfc * FFN_CHUNK), BF16)],
        compiler_params=pltpu.CompilerParams(dimension_semantics=("parallel", "parallel"),
                                             vmem_limit_bytes=VMEM_LIMIT),
        name="ffn",
    )(x, x, x, n2w, upg, upv, cwg, cwv, cbg, cbv, down, fw)


def kernel(x, norm1_w, w_in, pool_w, pool_scale, pool_out, qkv_conv_w, a_log, dt_bias, dn_norm_w,
           dn_out, w_o, norm2_w, ffn_up, ffn_conv_w, ffn_conv_b, ffn_down, final_norm_w):
    B, S, D = x.shape
    depth = w_in.shape[0]
    pool_dim = pool_out.shape[1]
    dn = dn_out.shape[1]
    heads = a_log.shape[-1]
    F = ffn_down.shape[1]
    assert depth == 1, "the final rmsnorm is fused into the (single) layer's channel mixer"
    assert dn == heads * HEAD_DIM and CHUNK == HEAD_DIM == LANES
    assert S % (CHUNK * DELTA_BATCH) == 0 and S // CHUNK <= LANES
    assert S % IN_TILE == 0 and S % MERGE_TILE == 0 and S % FFN_TILE == 0 and F % FFN_CHUNK == 0
    nfc = F // FFN_CHUNK
    N = S // CHUNK
    row = lambda t: t.reshape(1, -1).astype(F32)

    for l in range(depth):
        wl = w_in[l].astype(BF16)
        c0 = pool_dim
        wp = wl[:, :c0]
        wq, wk, wv, wz = (wl[:, c0 + i * dn:c0 + (i + 1) * dn] for i in range(4))
        c1 = c0 + 4 * dn
        wgt = jnp.pad(wl[:, c1:c1 + 4 * heads], ((0, 0), (0, LANES - 4 * heads)))
        wg = wl[:, c1 + 4 * heads:]

        q, k, v, pm, z, g, gates = _in_proj(
            x, row(norm1_w[l]), wq, wk, wv, wp, wz, wg, wgt, qkv_conv_w[l].astype(F32),
            pool_w[l].astype(BF16), row(pool_scale[l]), heads)

        gates_rows = gates[:, :, :4 * heads].reshape(B, N, CHUNK, 4, heads).transpose(0, 3, 4, 1, 2)
        nw_cols = jnp.broadcast_to(dn_norm_w[l].astype(F32)[:, None], (HEAD_DIM, LANES))
        o = _delta(q, k, v, z, gates_rows, a_log[l].astype(F32), dt_bias[l].astype(F32), nw_cols)

        x = _merge(o, pm, g, x, dn_out[l].astype(BF16), pool_out[l].astype(BF16), w_o[l].astype(BF16))

        up = ffn_up[l].astype(BF16)
        chunked = lambda t: t.reshape(t.shape[0], nfc, FFN_CHUNK).transpose(1, 0, 2)
        cw = ffn_conv_w[l].astype(F32)
        cb = ffn_conv_b[l].astype(F32).reshape(1, -1)
        x = _ffn(x, row(norm2_w[l]), chunked(up[:, :F]), chunked(up[:, F:]), chunked(cw[:, :F]),
                 chunked(cw[:, F:]), chunked(cb[:, :F]), chunked(cb[:, F:]), ffn_down[l].astype(BF16),
                 row(final_norm_w))
    return x
```

```python
import functools

import jax
import jax.numpy as jnp
from jax import lax
from jax.experimental import pallas as pl
from jax.experimental.pallas import tpu as pltpu

F32 = jnp.float32
BF16 = jnp.bfloat16

NORM_EPS = 1e-6
L2_EPS = 1e-6
POOL_WINDOWS = (2, 4, 8, 16)
POOL_GROUP_DIM = 128
HEAD_DIM = 128
SHORT_CONV = 5
FFN_CONV = 3

SUBLANES = 8
LANES = 128
HALO = SUBLANES
CHUNK = 128
DELTA_BATCH = 8
VMEM_LIMIT = 56 * 1024 * 1024

IN_TILE = 512
MERGE_TILE = 1024
FFN_TILE = 512
FFN_CHUNK = 256


def _shift_rows(a, k):
    n = a.shape[0]
    s = (-k) % n
    return a if s == 0 else pltpu.roll(a, s, axis=0)


def _dot(a, b):
    return jnp.dot(a, b, preferred_element_type=F32)


def _rms(xv, w):
    ms = jnp.mean(xv * xv, axis=-1, keepdims=True)
    return xv * lax.rsqrt(ms + NORM_EPS) * w


def _halo_valid(n_rows, tile, j, nj):
    row = lax.broadcasted_iota(jnp.int32, (n_rows, 1), 0)
    top_ok = jnp.logical_or(row >= HALO, j > 0)
    bot_ok = jnp.logical_or(row < tile + HALO, j < nj - 1)
    return jnp.logical_and(top_ok, bot_ok)


def _halo_specs(tile, width, seq):
    per = tile // HALO
    last = seq // HALO - 1
    prev = pl.BlockSpec((1, HALO, width), lambda b, j: (b, jnp.maximum(j * per - 1, 0), 0))
    main = pl.BlockSpec((1, tile, width), lambda b, j: (b, j, 0))
    nxt = pl.BlockSpec((1, HALO, width), lambda b, j: (b, jnp.minimum((j + 1) * per, last), 0))
    return prev, main, nxt


def _const_spec(shape):
    nd = len(shape)
    return pl.BlockSpec(shape, lambda *_: (0,) * nd, pipeline_mode=pl.Buffered(1))


def _in_proj_kernel(xp_ref, xm_ref, xn_ref, n1w_ref, wq_ref, wk_ref, wv_ref, wp_ref, wz_ref,
                    wg_ref, wgt_ref, cw_ref, poolw_ref, pools_ref,
                    q_out, k_out, v_out, pm_out, z_out, g_out, gt_out, *, tile, seq, heads):
    j = pl.program_id(1)
    nj = pl.num_programs(1)
    n = tile + 2 * HALO
    dn = heads * HEAD_DIM
    xe = jnp.concatenate([xp_ref[0], xm_ref[0], xn_ref[0]], axis=0)
    h = _rms(xe, n1w_ref[...])
    h = jnp.where(_halo_valid(n, tile, j, nj), h, 0.0)
    hb = h.astype(BF16)
    hm = h[HALO:HALO + tile].astype(BF16)

    def conv_silu(w_ref, col0):
        t = _dot(hb, w_ref[...])
        acc = None
        for tt in range(SHORT_CONV):
            sh = _shift_rows(t, tt - SHORT_CONV // 2)[HALO:HALO + tile]
            term = cw_ref[tt:tt + 1, col0:col0 + dn] * sh
            acc = term if acc is None else acc + term
        return acc * jax.nn.sigmoid(acc)

    def l2n(y, scale):
        outs = []
        for hh in range(heads):
            th = y[:, hh * HEAD_DIM:(hh + 1) * HEAD_DIM]
            ss = jnp.sum(th * th, axis=-1, keepdims=True)
            outs.append(th * (lax.rsqrt(ss + L2_EPS) * scale))
        return jnp.concatenate(outs, axis=-1)

    q_out[0] = l2n(conv_silu(wq_ref, 0), HEAD_DIM ** -0.5).astype(BF16)
    k_out[0] = l2n(conv_silu(wk_ref, dn), 1.0).astype(BF16)
    v_out[0] = conv_silu(wv_ref, 2 * dn).astype(BF16)

    pe = _dot(hb, wp_ref[...])
    pos = j * tile + lax.broadcasted_iota(jnp.int32, (tile, 1), 0)
    for g, w in enumerate(POOL_WINDOWS):
        zg = pe[:, g * POOL_GROUP_DIM:(g + 1) * POOL_GROUP_DIM]
        f = zg
        span = 1
        while span < w:
            f = f + _shift_rows(f, span)
            span *= 2
        ws = _shift_rows(f, -(w // 2))[HALO:HALO + tile]
        start = jnp.maximum(pos - w // 2, 0)
        end = jnp.minimum(pos + w - w // 2, seq)
        cnt = (end - start).astype(F32)
        pooled = ws / cnt - zg[HALO:HALO + tile]
        pmg = _dot(pooled.astype(BF16), poolw_ref[g])
        pmg = pmg * pools_ref[:, g * POOL_GROUP_DIM:(g + 1) * POOL_GROUP_DIM]
        pm_out[0, :, g * POOL_GROUP_DIM:(g + 1) * POOL_GROUP_DIM] = pmg.astype(BF16)

    z_out[0] = _dot(hm, wz_ref[...]).astype(BF16)
    g_out[0] = _dot(hm, wg_ref[...]).astype(BF16)
    gt_out[0] = _dot(hm, wgt_ref[...])


def _in_proj(x, n1w, wq, wk, wv, wp, wz, wg, wgt, cw, poolw, pools, heads):
    B, S, D = x.shape
    tile = IN_TILE
    dn = heads * HEAD_DIM
    pool_dim = wp.shape[1]
    xprev, xmain, xnext = _halo_specs(tile, D, S)
    out_spec = lambda width: pl.BlockSpec((1, tile, width), lambda b, j: (b, j, 0))
    return pl.pallas_call(
        functools.partial(_in_proj_kernel, tile=tile, seq=S, heads=heads),
        grid=(B, S // tile),
        in_specs=[xprev, xmain, xnext, _const_spec(n1w.shape), _const_spec(wq.shape),
                  _const_spec(wk.shape), _const_spec(wv.shape), _const_spec(wp.shape),
                  _const_spec(wz.shape), _const_spec(wg.shape), _const_spec(wgt.shape),
                  _const_spec(cw.shape), _const_spec(poolw.shape), _const_spec(pools.shape)],
        out_specs=[out_spec(dn), out_spec(dn), out_spec(dn), out_spec(pool_dim), out_spec(dn),
                   out_spec(wg.shape[1]), out_spec(LANES)],
        out_shape=[jax.ShapeDtypeStruct((B, S, dn), BF16)] * 3
        + [jax.ShapeDtypeStruct((B, S, pool_dim), BF16), jax.ShapeDtypeStruct((B, S, dn), BF16),
           jax.ShapeDtypeStruct((B, S, wg.shape[1]), BF16), jax.ShapeDtypeStruct((B, S, LANES), F32)],
        compiler_params=pltpu.CompilerParams(dimension_semantics=("parallel", "parallel"),
                                             vmem_limit_bytes=VMEM_LIMIT),
        name="in_proj",
    )(x, x, x, n1w, wq, wk, wv, wp, wz, wg, wgt, cw, poolw, pools)


def _softplus(t):
    return jnp.maximum(t, 0.0) + jnp.log1p(jnp.exp(-jnp.abs(t)))


def _scan_sum(a, axis, reverse):
    n = a.shape[axis]
    idx = lax.broadcasted_iota(jnp.int32, a.shape, axis)
    s = 1
    while s < n:
        if reverse:
            a = a + jnp.where(idx < n - s, pltpu.roll(a, n - s, axis=axis), 0.0)
        else:
            a = a + jnp.where(idx >= s, pltpu.roll(a, s, axis=axis), 0.0)
        s *= 2
    return a


def _bmm(a, b):
    return jnp.einsum("uik,ukj->uij", a, b, preferred_element_type=F32)


def _bmm_nt(a, b):
    return jnp.einsum("uik,ujk->uij", a, b, preferred_element_type=F32)


def _delta_kernel(alog_ref, dtb_ref, q_ref, k_ref, v_ref, z_ref, gates_ref, nwc_ref, o_ref,
                  grow_s, brow_s, rhs1_a, rhs2_a, ut_a, gl_a, rhs1_b, rhs2_b, ut_b, gl_b, of_s, ob_s,
                  *, nchunks, batch):
    C = CHUNK
    N = nchunks
    U = batch
    D = HEAD_DIM
    stages = N // U
    hh = pl.program_id(1)

    for d in range(2):
        a = jnp.exp(jnp.full((1, 1), alog_ref[d, hh], F32))
        g_r = -a * _softplus(gates_ref[0, 2 + d, 0] + dtb_ref[d, hh])
        grow_s[d] = _scan_sum(g_r, 1, d == 1).reshape(N, 1, C)
        brow_s[d] = jax.nn.sigmoid(gates_ref[0, d, 0]).reshape(N, 1, C)

    rr = lax.broadcasted_iota(jnp.int32, (U, C, C), 1)
    cc = lax.broadcasted_iota(jnp.int32, (U, C, C), 2)

    def prep(g, bufs):
        rhs1_s, rhs2_s, ut_s, gl_s = bufs
        for d in range(2):
            n0 = pl.multiple_of(g * U if d == 0 else N - (g + 1) * U, U)
            r0 = pl.multiple_of(n0 * C, U * C)
            k = k_ref[0, pl.ds(r0, U * C), :].reshape(U, C, D)
            q = q_ref[0, pl.ds(r0, U * C), :].reshape(U, C, D)
            kf = k.astype(F32)
            kT = jnp.swapaxes(kf, 1, 2)
            qT = jnp.swapaxes(q.astype(F32), 1, 2)
            vT = jnp.swapaxes(v_ref[0, pl.ds(r0, U * C), :].reshape(U, C, D).astype(F32), 1, 2)
            kkq = _bmm_nt(k, jnp.concatenate([k, q], axis=1))
            KK = kkq[:, :, :C]
            QKt = kkq[:, :, C:]
            GR = grow_s[d, pl.ds(n0, U)]
            BR = brow_s[d, pl.ds(n0, U)]
            GC = jnp.swapaxes(jnp.broadcast_to(GR, (U, C, C)), 1, 2)
            incl = (cc >= rr) if d == 0 else (cc <= rr)
            strict = (cc > rr) if d == 0 else (cc < rr)
            decay = jnp.where(incl, jnp.exp(jnp.minimum(GR - GC, 0.0)), 0.0)
            Lt = jnp.where(strict, KK * BR * decay, 0.0)
            attn = QKt * decay
            Y = -Lt
            Lb = Lt.astype(BF16)
            P = _bmm(Lb, Lb)
            rounds = C.bit_length() - 2
            for r in range(rounds):
                Pb = P.astype(BF16)
                if r < rounds - 1:
                    py = _bmm(Pb, jnp.concatenate([Y.astype(BF16), Pb], axis=2))
                    Y = Y + P + py[:, :, :C]
                    P = py[:, :, C:]
                else:
                    Y = Y + P + _bmm(Pb, Y.astype(BF16))
            eG = jnp.exp(GR)
            X = jnp.concatenate([kT * (BR * eG), vT * BR], axis=1)
            wu = X + _bmm(X.astype(BF16), Y.astype(BF16))
            g_last = GC[:, C - 1:C, :] if d == 0 else GC[:, 0:1, :]
            kd = kf * jnp.exp(g_last - GC)
            rhs1_s[d] = jnp.concatenate([wu[:, :D], qT * eG], axis=2).astype(BF16)
            rhs2_s[d] = jnp.concatenate([attn, kd], axis=2).astype(BF16)
            ut_s[d] = wu[:, D:]
            gl_s[d] = jnp.exp(g_last)

    def serial(g, bufs, states):
        rhs1_s, rhs2_s, ut_s, gl_s = bufs
        for j in range(U):
            new_states = []
            for d in range(2):
                jd = j if d == 0 else U - 1 - j
                n = g * U + j if d == 0 else N - 1 - (g * U + j)
                ST = states[d]
                p1 = _dot(ST.astype(BF16), rhs1_s[d, jd])
                vn = ut_s[d, jd] - p1[:, :C]
                p2 = _dot(vn.astype(BF16), rhs2_s[d, jd])
                (of_s if d == 0 else ob_s)[n] = p1[:, C:] + p2[:, :C]
                new_states.append(ST * gl_s[d, jd] + p2[:, C:])
            states = tuple(new_states)
        return states

    bufs_a = (rhs1_a, rhs2_a, ut_a, gl_a)
    bufs_b = (rhs1_b, rhs2_b, ut_b, gl_b)
    zero = jnp.zeros((D, D), F32)
    prep(0, bufs_a)

    def two_stages(i, states):
        prep(2 * i + 1, bufs_b)
        states = serial(2 * i, bufs_a, states)
        prep(2 * i + 2, bufs_a)
        return serial(2 * i + 1, bufs_b, states)

    states = lax.fori_loop(0, stages // 2 - 1, two_stages, (zero, zero))
    prep(stages - 1, bufs_b)
    states = serial(stages - 2, bufs_a, states)
    serial(stages - 1, bufs_b, states)

    def fin(n, carry):
        r0 = pl.multiple_of(n * C, C)
        ot = of_s[n] + ob_s[n]
        ms = jnp.mean(ot * ot, axis=0, keepdims=True)
        o = (ot * lax.rsqrt(ms + NORM_EPS) * nwc_ref[...]).T
        zz = z_ref[0, pl.ds(r0, C), :].astype(F32)
        o_ref[0, pl.ds(r0, C), :] = (o * (zz * jax.nn.sigmoid(zz))).astype(BF16)
        return carry

    lax.fori_loop(0, N, fin, 0, unroll=4)


def _delta(q, k, v, z, gates_rows, a_log, dt_bias, nw_cols):
    B, S, dn = q.shape
    heads = dn // HEAD_DIM
    N = S // CHUNK
    U = DELTA_BATCH
    assert N % (2 * U) == 0 and N // U >= 2
    head_spec = pl.BlockSpec((1, S, HEAD_DIM), lambda b, h: (b, 0, h))
    smem = pl.BlockSpec(memory_space=pltpu.SMEM)
    return pl.pallas_call(
        functools.partial(_delta_kernel, nchunks=N, batch=DELTA_BATCH),
        grid=(B, heads),
        in_specs=[smem, smem, head_spec, head_spec, head_spec, head_spec,
                  pl.BlockSpec((1, 4, 1, N, CHUNK), lambda b, h: (b, 0, h, 0, 0)),
                  pl.BlockSpec((HEAD_DIM, LANES), lambda b, h: (0, 0))],
        out_specs=head_spec,
        out_shape=jax.ShapeDtypeStruct((B, S, dn), BF16),
        scratch_shapes=[
            pltpu.VMEM((2, N, 1, CHUNK), F32),
            pltpu.VMEM((2, N, 1, CHUNK), F32),
        ] + 2 * [
            pltpu.VMEM((2, U, HEAD_DIM, 2 * CHUNK), BF16),
            pltpu.VMEM((2, U, CHUNK, CHUNK + HEAD_DIM), BF16),
            pltpu.VMEM((2, U, HEAD_DIM, CHUNK), F32),
            pltpu.VMEM((2, U, 1, LANES), F32),
        ] + 2 * [pltpu.VMEM((N, HEAD_DIM, CHUNK), F32)],
        compiler_params=pltpu.CompilerParams(dimension_semantics=("parallel", "parallel"),
                                             vmem_limit_bytes=VMEM_LIMIT),
        name="delta_rule",
    )(a_log, dt_bias, q, k, v, z, gates_rows, nw_cols)


def _merge_kernel(o_ref, pm_ref, g_ref, x_ref, dn_ref, po_ref, wo_ref, out_ref):
    d = out_ref.shape[-1]
    y_dn = _dot(o_ref[0], dn_ref[...])
    y_pool = _dot(pm_ref[0], po_ref[...])
    g = g_ref[0].astype(F32)
    merged = jax.nn.sigmoid(g[:, :d]) * y_pool + jax.nn.sigmoid(g[:, d:]) * y_dn
    out_ref[0] = x_ref[0] + _dot(merged.astype(BF16), wo_ref[...])


def _merge(o, pm, g, x, dn_out, pool_out, w_o):
    B, S, D = x.shape
    tile = MERGE_TILE
    spec = lambda width: pl.BlockSpec((1, tile, width), lambda b, j: (b, j, 0))
    return pl.pallas_call(
        _merge_kernel,
        grid=(B, S // tile),
        in_specs=[spec(o.shape[-1]), spec(pm.shape[-1]), spec(g.shape[-1]), spec(D),
                  _const_spec(dn_out.shape), _const_spec(pool_out.shape), _const_spec(w_o.shape)],
        out_specs=spec(D),
        out_shape=jax.ShapeDtypeStruct((B, S, D), F32),
        compiler_params=pltpu.CompilerParams(dimension_semantics=("parallel", "parallel"),
                                             vmem_limit_bytes=VMEM_LIMIT),
        name="merge",
    )(o, pm, g, x, dn_out, pool_out, w_o)


def _ffn_kernel(xp_ref, xm_ref, xn_ref, n2w_ref, upg_ref, upv_ref, cwg_ref, cwv_ref, cbg_ref,
                cbv_ref, down_ref, fw_ref, out_ref, act_s, *, tile, nfc):
    j = pl.program_id(1)
    nj = pl.num_programs(1)
    n = tile + 2 * HALO
    xm = xm_ref[0]
    xe = jnp.concatenate([xp_ref[0], xm, xn_ref[0]], axis=0)
    h = _rms(xe, n2w_ref[...])
    hb = jnp.where(_halo_valid(n, tile, j, nj), h, 0.0).astype(BF16)

    def conv(t, cw_ref, cb_ref, c):
        acc = None
        for tt in range(FFN_CONV):
            sh = _shift_rows(t, tt - FFN_CONV // 2)[HALO:HALO + tile]
            term = cw_ref[c, tt:tt + 1, :] * sh
            acc = term if acc is None else acc + term
        return acc + cb_ref[c]

    for c in range(nfc):
        gate = conv(_dot(hb, upg_ref[c]), cwg_ref, cbg_ref, c)
        val = conv(_dot(hb, upv_ref[c]), cwv_ref, cbv_ref, c)
        act_s[:, c * FFN_CHUNK:(c + 1) * FFN_CHUNK] = (gate * jax.nn.sigmoid(gate) * val).astype(BF16)

    x2 = xm + _dot(act_s[...], down_ref[...])
    out_ref[0] = _rms(x2, fw_ref[...])


def _ffn(x, n2w, upg, upv, cwg, cwv, cbg, cbv, down, fw):
    B, S, D = x.shape
    tile = FFN_TILE
    nfc = upg.shape[0]
    xprev, xmain, xnext = _halo_specs(tile, D, S)
    return pl.pallas_call(
        functools.partial(_ffn_kernel, tile=tile, nfc=nfc),
        grid=(B, S // tile),
        in_specs=[xprev, xmain, xnext, _const_spec(n2w.shape), _const_spec(upg.shape),
                  _const_spec(upv.shape), _const_spec(cwg.shape), _const_spec(cwv.shape),
                  _const_spec(cbg.shape), _const_spec(cbv.shape), _const_spec(down.shape),
                  _const_spec(fw.shape)],
        out_specs=pl.BlockSpec((1, tile, D), lambda b, j: (b, j, 0)),
        out_shape=jax.ShapeDtypeStruct((B, S, D), F32),
        scratch_shapes=[pltpu.VMEM((tile, nfc * FFN_CHUNK), BF16)],
        compiler_params=pltpu.CompilerParams(dimension_semantics=("parallel", "parallel"),
                                             vmem_limit_bytes=VMEM_LIMIT),
        name="ffn",
    )(x, x, x, n2w, upg, upv, cwg, cwv, cbg, cbv, down, fw)


def kernel(x, norm1_w, w_in, pool_w, pool_scale, pool_out, qkv_conv_w, a_log, dt_bias, dn_norm_w,
           dn_out, w_o, norm2_w, ffn_up, ffn_conv_w, ffn_conv_b, ffn_down, final_norm_w):
    B, S, D = x.shape
    depth = w_in.shape[0]
    pool_dim = pool_out.shape[1]
    dn = dn_out.shape[1]
    heads = a_log.shape[-1]
    F = ffn_down.shape[1]
    assert depth == 1, "the final rmsnorm is fused into the (single) layer's channel mixer"
    assert dn == heads * HEAD_DIM and CHUNK == HEAD_DIM == LANES
    assert S % (CHUNK * DELTA_BATCH) == 0 and S // CHUNK <= LANES
    assert S % IN_TILE == 0 and S % MERGE_TILE == 0 and S % FFN_TILE == 0 and F % FFN_CHUNK == 0
    nfc = F // FFN_CHUNK
    N = S // CHUNK
    row = lambda t: t.reshape(1, -1).astype(F32)

    for l in range(depth):
        wl = w_in[l].astype(BF16)
        c0 = pool_dim
        wp = wl[:, :c0]
        wq, wk, wv, wz = (wl[:, c0 + i * dn:c0 + (i + 1) * dn] for i in range(4))
        c1 = c0 + 4 * dn
        wgt = jnp.pad(wl[:, c1:c1 + 4 * heads], ((0, 0), (0, LANES - 4 * heads)))
        wg = wl[:, c1 + 4 * heads:]

        q, k, v, pm, z, g, gates = _in_proj(
            x, row(norm1_w[l]), wq, wk, wv, wp, wz, wg, wgt, qkv_conv_w[l].astype(F32),
            pool_w[l].astype(BF16), row(pool_scale[l]), heads)

        gates_rows = gates[:, :, :4 * heads].reshape(B, N, CHUNK, 4, heads).transpose(0, 3, 4, 1, 2)
        nw_cols = jnp.broadcast_to(dn_norm_w[l].astype(F32)[:, None], (HEAD_DIM, LANES))
        o = _delta(q, k, v, z, gates_rows, a_log[l].astype(F32), dt_bias[l].astype(F32), nw_cols)

        x = _merge(o, pm, g, x, dn_out[l].astype(BF16), pool_out[l].astype(BF16), w_o[l].astype(BF16))

        up = ffn_up[l].astype(BF16)
        chunked = lambda t: t.reshape(t.shape[0], nfc, FFN_CHUNK).transpose(1, 0, 2)
        cw = ffn_conv_w[l].astype(F32)
        cb = ffn_conv_b[l].astype(F32).reshape(1, -1)
        x = _ffn(x, row(norm2_w[l]), chunked(up[:, :F]), chunked(up[:, F:]), chunked(cw[:, :F]),
                 chunked(cw[:, F:]), chunked(cb[:, :F]), chunked(cb[:, F:]), ffn_down[l].astype(BF16),
                 row(final_norm_w))
    return x
```

```python
import functools

import jax
import jax.numpy as jnp
from jax import lax
from jax.experimental import pallas as pl
from jax.experimental.pallas import tpu as pltpu

F32 = jnp.float32
BF16 = jnp.bfloat16

NORM_EPS = 1e-6
L2_EPS = 1e-6
POOL_WINDOWS = (2, 4, 8, 16)
POOL_GROUP_DIM = 128
HEAD_DIM = 128
SHORT_CONV = 5
FFN_CONV = 3

SUBLANES = 8
LANES = 128
HALO = SUBLANES
CHUNK = 128
DELTA_BATCH = 8
VMEM_LIMIT = 56 * 1024 * 1024

IN_TILE = 512
MERGE_TILE = 1024
FFN_TILE = 512
FFN_CHUNK = 256


def _interleave(scr, val):
    n, width = val.shape
    pitch = n // SUBLANES
    blocks = width // LANES
    for c in range(blocks):
        scr[c] = val[:, c * LANES:(c + 1) * LANES]
    slabs = [jnp.concatenate([scr[c, pl.ds(a, SUBLANES, stride=pitch), :] for c in range(blocks)], axis=1)
             for a in range(pitch)]
    return jnp.concatenate(slabs, axis=0)


def _deinterleave(scr, val, lo, hi):
    n, width = val.shape
    pitch = n // SUBLANES
    blocks = width // LANES
    for a in range(pitch):
        for c in range(blocks):
            scr[c, pl.ds(a, SUBLANES, stride=pitch), :] = val[a * SUBLANES:(a + 1) * SUBLANES,
                                                             c * LANES:(c + 1) * LANES]
    return jnp.concatenate([scr[c, lo:hi, :] for c in range(blocks)], axis=1)


def _shift_interleaved(t, s):
    n = t.shape[0]
    if s == 0:
        return t
    m = abs(s)
    if s > 0:
        edge = [pltpu.roll(t[SUBLANES * j:SUBLANES * (j + 1)], SUBLANES - 1, axis=0) for j in range(m)]
        return jnp.concatenate([t[SUBLANES * m:]] + edge, axis=0)
    edge = [pltpu.roll(t[n - SUBLANES * j:n - SUBLANES * (j - 1)], 1, axis=0) for j in range(m, 0, -1)]
    return jnp.concatenate(edge + [t[:n - SUBLANES * m]], axis=0)


def _dot(a, b):
    return jnp.dot(a, b, preferred_element_type=F32)


def _rms(xv, w):
    ms = jnp.mean(xv * xv, axis=-1, keepdims=True)
    return xv * lax.rsqrt(ms + NORM_EPS) * w


def _halo_valid(n_rows, tile, j, nj):
    row = lax.broadcasted_iota(jnp.int32, (n_rows, 1), 0)
    top_ok = jnp.logical_or(row >= HALO, j > 0)
    bot_ok = jnp.logical_or(row < tile + HALO, j < nj - 1)
    return jnp.logical_and(top_ok, bot_ok)


def _halo_specs(tile, width, seq):
    per = tile // HALO
    last = seq // HALO - 1
    prev = pl.BlockSpec((1, HALO, width), lambda b, j: (b, jnp.maximum(j * per - 1, 0), 0))
    main = pl.BlockSpec((1, tile, width), lambda b, j: (b, j, 0))
    nxt = pl.BlockSpec((1, HALO, width), lambda b, j: (b, jnp.minimum((j + 1) * per, last), 0))
    return prev, main, nxt


def _const_spec(shape):
    nd = len(shape)
    return pl.BlockSpec(shape, lambda *_: (0,) * nd, pipeline_mode=pl.Buffered(1))


def _in_proj_kernel(xp_ref, xm_ref, xn_ref, n1w_ref, wq_ref, wk_ref, wv_ref, wp_ref, wz_ref,
                    wg_ref, wgt_ref, cw_ref, poolw_ref, pools_ref,
                    q_out, k_out, v_out, pm_out, z_out, g_out, gt_out, il_s, *, tile, seq, heads):
    j = pl.program_id(1)
    nj = pl.num_programs(1)
    n = tile + 2 * HALO
    dn = heads * HEAD_DIM
    xe = jnp.concatenate([xp_ref[0], xm_ref[0], xn_ref[0]], axis=0)
    h = _rms(xe, n1w_ref[...])
    h = jnp.where(_halo_valid(n, tile, j, nj), h, 0.0)
    hm = h[HALO:HALO + tile].astype(BF16)
    hi = _interleave(il_s, h).astype(BF16)

    def conv_silu(w_ref, col0):
        t = _dot(hi, w_ref[...])
        acc = None
        for tt in range(SHORT_CONV):
            term = cw_ref[tt:tt + 1, col0:col0 + dn] * _shift_interleaved(t, tt - SHORT_CONV // 2)
            acc = term if acc is None else acc + term
        return acc * jax.nn.sigmoid(acc)

    def l2n(y, scale):
        outs = []
        for hh in range(heads):
            th = y[:, hh * HEAD_DIM:(hh + 1) * HEAD_DIM]
            ss = jnp.sum(th * th, axis=-1, keepdims=True)
            outs.append(th * (lax.rsqrt(ss + L2_EPS) * scale))
        return jnp.concatenate(outs, axis=-1)

    def natural(y):
        return _deinterleave(il_s, y, HALO, HALO + tile).astype(BF16)

    q_out[0] = natural(l2n(conv_silu(wq_ref, 0), HEAD_DIM ** -0.5))
    k_out[0] = natural(l2n(conv_silu(wk_ref, dn), 1.0))
    v_out[0] = natural(conv_silu(wv_ref, 2 * dn))

    pe = _dot(hi, wp_ref[...])
    slot = lax.broadcasted_iota(jnp.int32, (n, 1), 0)
    pos = j * tile - HALO + (slot // SUBLANES) + (n // SUBLANES) * (slot % SUBLANES)
    pms = []
    for g, w in enumerate(POOL_WINDOWS):
        zg = pe[:, g * POOL_GROUP_DIM:(g + 1) * POOL_GROUP_DIM]
        f = zg
        span = 1
        while span < w:
            f = f + _shift_interleaved(f, span)
            span *= 2
        ws = _shift_interleaved(f, -(w // 2))
        start = jnp.maximum(pos - w // 2, 0)
        end = jnp.minimum(pos + w - w // 2, seq)
        cnt = jnp.maximum(end - start, 1).astype(F32)
        pooled = ws / cnt - zg
        pmg = _dot(pooled.astype(BF16), poolw_ref[g])
        pms.append(pmg * pools_ref[:, g * POOL_GROUP_DIM:(g + 1) * POOL_GROUP_DIM])
    pm_out[0] = natural(jnp.concatenate(pms, axis=1))

    z_out[0] = _dot(hm, wz_ref[...]).astype(BF16)
    g_out[0] = _dot(hm, wg_ref[...]).astype(BF16)
    gt_out[0] = _dot(hm, wgt_ref[...])


def _in_proj(x, n1w, wq, wk, wv, wp, wz, wg, wgt, cw, poolw, pools, heads):
    B, S, D = x.shape
    tile = IN_TILE
    dn = heads * HEAD_DIM
    pool_dim = wp.shape[1]
    xprev, xmain, xnext = _halo_specs(tile, D, S)
    out_spec = lambda width: pl.BlockSpec((1, tile, width), lambda b, j: (b, j, 0))
    return pl.pallas_call(
        functools.partial(_in_proj_kernel, tile=tile, seq=S, heads=heads),
        grid=(B, S // tile),
        in_specs=[xprev, xmain, xnext, _const_spec(n1w.shape), _const_spec(wq.shape),
                  _const_spec(wk.shape), _const_spec(wv.shape), _const_spec(wp.shape),
                  _const_spec(wz.shape), _const_spec(wg.shape), _const_spec(wgt.shape),
                  _const_spec(cw.shape), _const_spec(poolw.shape), _const_spec(pools.shape)],
        out_specs=[out_spec(dn), out_spec(dn), out_spec(dn), out_spec(pool_dim), out_spec(dn),
                   out_spec(wg.shape[1]), out_spec(LANES)],
        out_shape=[jax.ShapeDtypeStruct((B, S, dn), BF16)] * 3
        + [jax.ShapeDtypeStruct((B, S, pool_dim), BF16), jax.ShapeDtypeStruct((B, S, dn), BF16),
           jax.ShapeDtypeStruct((B, S, wg.shape[1]), BF16), jax.ShapeDtypeStruct((B, S, LANES), F32)],
        scratch_shapes=[pltpu.VMEM((max(D, dn) // LANES, tile + 2 * HALO, LANES), F32)],
        compiler_params=pltpu.CompilerParams(dimension_semantics=("parallel", "parallel"),
                                             vmem_limit_bytes=VMEM_LIMIT),
        name="in_proj",
    )(x, x, x, n1w, wq, wk, wv, wp, wz, wg, wgt, cw, poolw, pools)


def _softplus(t):
    return jnp.maximum(t, 0.0) + jnp.log1p(jnp.exp(-jnp.abs(t)))


def _scan_sum(a, axis, reverse):
    n = a.shape[axis]
    idx = lax.broadcasted_iota(jnp.int32, a.shape, axis)
    s = 1
    while s < n:
        if reverse:
            a = a + jnp.where(idx < n - s, pltpu.roll(a, n - s, axis=axis), 0.0)
        else:
            a = a + jnp.where(idx >= s, pltpu.roll(a, s, axis=axis), 0.0)
        s *= 2
    return a


def _bmm(a, b):
    return jnp.einsum("uik,ukj->uij", a, b, preferred_element_type=F32)


def _bmm_nt(a, b):
    return jnp.einsum("uik,ujk->uij", a, b, preferred_element_type=F32)


def _delta_kernel(alog_ref, dtb_ref, q_ref, k_ref, v_ref, z_ref, gates_ref, nwc_ref, o_ref,
                  grow_s, brow_s, rhs1_a, rhs2_a, ut_a, gl_a, rhs1_b, rhs2_b, ut_b, gl_b, of_s, ob_s,
                  *, nchunks, batch):
    C = CHUNK
    N = nchunks
    U = batch
    D = HEAD_DIM
    stages = N // U
    hh = pl.program_id(1)

    for d in range(2):
        a = jnp.exp(jnp.full((1, 1), alog_ref[d, hh], F32))
        g_r = -a * _softplus(gates_ref[0, 2 + d, 0] + dtb_ref[d, hh])
        grow_s[d] = _scan_sum(g_r, 1, d == 1).reshape(N, 1, C)
        brow_s[d] = jax.nn.sigmoid(gates_ref[0, d, 0]).reshape(N, 1, C)

    rr = lax.broadcasted_iota(jnp.int32, (U, C, C), 1)
    cc = lax.broadcasted_iota(jnp.int32, (U, C, C), 2)

    def prep(g, bufs):
        rhs1_s, rhs2_s, ut_s, gl_s = bufs
        for d in range(2):
            n0 = pl.multiple_of(g * U if d == 0 else N - (g + 1) * U, U)
            r0 = pl.multiple_of(n0 * C, U * C)
            k = k_ref[0, pl.ds(r0, U * C), :].reshape(U, C, D)
            q = q_ref[0, pl.ds(r0, U * C), :].reshape(U, C, D)
            kf = k.astype(F32)
            kT = jnp.swapaxes(kf, 1, 2)
            qT = jnp.swapaxes(q.astype(F32), 1, 2)
            vT = jnp.swapaxes(v_ref[0, pl.ds(r0, U * C), :].reshape(U, C, D).astype(F32), 1, 2)
            kkq = _bmm_nt(k, jnp.concatenate([k, q], axis=1))
            KK = kkq[:, :, :C]
            QKt = kkq[:, :, C:]
            GR = grow_s[d, pl.ds(n0, U)]
            BR = brow_s[d, pl.ds(n0, U)]
            GC = jnp.swapaxes(jnp.broadcast_to(GR, (U, C, C)), 1, 2)
            incl = (cc >= rr) if d == 0 else (cc <= rr)
            strict = (cc > rr) if d == 0 else (cc < rr)
            decay = jnp.where(incl, jnp.exp(jnp.minimum(GR - GC, 0.0)), 0.0)
            Lt = jnp.where(strict, KK * BR * decay, 0.0)
            attn = QKt * decay
            Y = -Lt
            Lb = Lt.astype(BF16)
            P = _bmm(Lb, Lb)
            rounds = C.bit_length() - 2
            for r in range(rounds):
                Pb = P.astype(BF16)
                if r < rounds - 1:
                    py = _bmm(Pb, jnp.concatenate([Y.astype(BF16), Pb], axis=2))
                    Y = Y + P + py[:, :, :C]
                    P = py[:, :, C:]
                else:
                    Y = Y + P + _bmm(Pb, Y.astype(BF16))
            eG = jnp.exp(GR)
            X = jnp.concatenate([kT * (BR * eG), vT * BR], axis=1)
            wu = X + _bmm(X.astype(BF16), Y.astype(BF16))
            g_last = GC[:, C - 1:C, :] if d == 0 else GC[:, 0:1, :]
            kd = kf * jnp.exp(g_last - GC)
            rhs1_s[d] = jnp.concatenate([wu[:, :D], qT * eG], axis=2).astype(BF16)
            rhs2_s[d] = jnp.concatenate([attn, kd], axis=2).astype(BF16)
            ut_s[d] = wu[:, D:]
            gl_s[d] = jnp.exp(g_last)

    def serial(g, bufs, states):
        rhs1_s, rhs2_s, ut_s, gl_s = bufs
        for j in range(U):
            new_states = []
            for d in range(2):
                jd = j if d == 0 else U - 1 - j
                n = g * U + j if d == 0 else N - 1 - (g * U + j)
                ST = states[d]
                p1 = _dot(ST.astype(BF16), rhs1_s[d, jd])
                vn = ut_s[d, jd] - p1[:, :C]
                p2 = _dot(vn.astype(BF16), rhs2_s[d, jd])
                (of_s if d == 0 else ob_s)[n] = p1[:, C:] + p2[:, :C]
                new_states.append(ST * gl_s[d, jd] + p2[:, C:])
            states = tuple(new_states)
        return states

    bufs_a = (rhs1_a, rhs2_a, ut_a, gl_a)
    bufs_b = (rhs1_b, rhs2_b, ut_b, gl_b)
    zero = jnp.zeros((D, D), F32)
    prep(0, bufs_a)

    def two_stages(i, states):
        prep(2 * i + 1, bufs_b)
        states = serial(2 * i, bufs_a, states)
        prep(2 * i + 2, bufs_a)
        return serial(2 * i + 1, bufs_b, states)

    states = lax.fori_loop(0, stages // 2 - 1, two_stages, (zero, zero))
    prep(stages - 1, bufs_b)
    states = serial(stages - 2, bufs_a, states)
    serial(stages - 1, bufs_b, states)

    def fin(n, carry):
        r0 = pl.multiple_of(n * C, C)
        ot = of_s[n] + ob_s[n]
        ms = jnp.mean(ot * ot, axis=0, keepdims=True)
        o = (ot * lax.rsqrt(ms + NORM_EPS) * nwc_ref[...]).T
        zz = z_ref[0, pl.ds(r0, C), :].astype(F32)
        o_ref[0, pl.ds(r0, C), :] = (o * (zz * jax.nn.sigmoid(zz))).astype(BF16)
        return carry

    lax.fori_loop(0, N, fin, 0, unroll=4)


def _delta(q, k, v, z, gates_rows, a_log, dt_bias, nw_cols):
    B, S, dn = q.shape
    heads = dn // HEAD_DIM
    N = S // CHUNK
    U = DELTA_BATCH
    assert N % (2 * U) == 0 and N // U >= 2
    head_spec = pl.BlockSpec((1, S, HEAD_DIM), lambda b, h: (b, 0, h))
    smem = pl.BlockSpec(memory_space=pltpu.SMEM)
    return pl.pallas_call(
        functools.partial(_delta_kernel, nchunks=N, batch=DELTA_BATCH),
        grid=(B, heads),
        in_specs=[smem, smem, head_spec, head_spec, head_spec, head_spec,
                  pl.BlockSpec((1, 4, 1, N, CHUNK), lambda b, h: (b, 0, h, 0, 0)),
                  pl.BlockSpec((HEAD_DIM, LANES), lambda b, h: (0, 0))],
        out_specs=head_spec,
        out_shape=jax.ShapeDtypeStruct((B, S, dn), BF16),
        scratch_shapes=[
            pltpu.VMEM((2, N, 1, CHUNK), F32),
            pltpu.VMEM((2, N, 1, CHUNK), F32),
        ] + 2 * [
            pltpu.VMEM((2, U, HEAD_DIM, 2 * CHUNK), BF16),
            pltpu.VMEM((2, U, CHUNK, CHUNK + HEAD_DIM), BF16),
            pltpu.VMEM((2, U, HEAD_DIM, CHUNK), F32),
            pltpu.VMEM((2, U, 1, LANES), F32),
        ] + 2 * [pltpu.VMEM((N, HEAD_DIM, CHUNK), F32)],
        compiler_params=pltpu.CompilerParams(dimension_semantics=("parallel", "parallel"),
                                             vmem_limit_bytes=VMEM_LIMIT),
        name="delta_rule",
    )(a_log, dt_bias, q, k, v, z, gates_rows, nw_cols)


def _merge_kernel(o_ref, pm_ref, g_ref, x_ref, dn_ref, po_ref, wo_ref, out_ref):
    d = out_ref.shape[-1]
    y_dn = _dot(o_ref[0], dn_ref[...])
    y_pool = _dot(pm_ref[0], po_ref[...])
    g = g_ref[0].astype(F32)
    merged = jax.nn.sigmoid(g[:, :d]) * y_pool + jax.nn.sigmoid(g[:, d:]) * y_dn
    out_ref[0] = x_ref[0] + _dot(merged.astype(BF16), wo_ref[...])


def _merge(o, pm, g, x, dn_out, pool_out, w_o):
    B, S, D = x.shape
    tile = MERGE_TILE
    spec = lambda width: pl.BlockSpec((1, tile, width), lambda b, j: (b, j, 0))
    return pl.pallas_call(
        _merge_kernel,
        grid=(B, S // tile),
        in_specs=[spec(o.shape[-1]), spec(pm.shape[-1]), spec(g.shape[-1]), spec(D),
                  _const_spec(dn_out.shape), _const_spec(pool_out.shape), _const_spec(w_o.shape)],
        out_specs=spec(D),
        out_shape=jax.ShapeDtypeStruct((B, S, D), F32),
        compiler_params=pltpu.CompilerParams(dimension_semantics=("parallel", "parallel"),
                                             vmem_limit_bytes=VMEM_LIMIT),
        name="merge",
    )(o, pm, g, x, dn_out, pool_out, w_o)


def _ffn_kernel(xp_ref, xm_ref, xn_ref, n2w_ref, upg_ref, upv_ref, cwg_ref, cwv_ref, cbg_ref,
                cbv_ref, down_ref, fw_ref, out_ref, act_s, il_s, *, tile, nfc):
    j = pl.program_id(1)
    nj = pl.num_programs(1)
    n = tile + 2 * HALO
    xm = xm_ref[0]
    xe = jnp.concatenate([xp_ref[0], xm, xn_ref[0]], axis=0)
    h = jnp.where(_halo_valid(n, tile, j, nj), _rms(xe, n2w_ref[...]), 0.0)
    hb = _interleave(il_s, h).astype(BF16)

    def conv(u, cw_ref, cb_ref, c):
        acc = None
        for tt in range(FFN_CONV):
            term = cw_ref[c, tt:tt + 1, :] * _shift_interleaved(u, tt - FFN_CONV // 2)
            acc = term if acc is None else acc + term
        return acc + cb_ref[c]

    for c in range(nfc):
        gate = conv(_dot(hb, upg_ref[c]), cwg_ref, cbg_ref, c)
        val = conv(_dot(hb, upv_ref[c]), cwv_ref, cbv_ref, c)
        act_s[:, c * FFN_CHUNK:(c + 1) * FFN_CHUNK] = (gate * jax.nn.sigmoid(gate) * val).astype(BF16)

    y = _deinterleave(il_s, _dot(act_s[...], down_ref[...]), HALO, HALO + tile)
    out_ref[0] = _rms(xm + y, fw_ref[...])


def _ffn(x, n2w, upg, upv, cwg, cwv, cbg, cbv, down, fw):
    B, S, D = x.shape
    tile = FFN_TILE
    nfc = upg.shape[0]
    xprev, xmain, xnext = _halo_specs(tile, D, S)
    return pl.pallas_call(
        functools.partial(_ffn_kernel, tile=tile, nfc=nfc),
        grid=(B, S // tile),
        in_specs=[xprev, xmain, xnext, _const_spec(n2w.shape), _const_spec(upg.shape),
                  _const_spec(upv.shape), _const_spec(cwg.shape), _const_spec(cwv.shape),
                  _const_spec(cbg.shape), _const_spec(cbv.shape), _const_spec(down.shape),
                  _const_spec(fw.shape)],
        out_specs=pl.BlockSpec((1, tile, D), lambda b, j: (b, j, 0)),
        out_shape=jax.ShapeDtypeStruct((B, S, D), F32),
        scratch_shapes=[pltpu.VMEM((tile + 2 * HALO, nfc * FFN_CHUNK), BF16),
                        pltpu.VMEM((D // LANES, tile + 2 * HALO, LANES), F32)],
        compiler_params=pltpu.CompilerParams(dimension_semantics=("parallel", "parallel"),
                                             vmem_limit_bytes=VMEM_LIMIT),
        name="ffn",
    )(x, x, x, n2w, upg, upv, cwg, cwv, cbg, cbv, down, fw)


def kernel(x, norm1_w, w_in, pool_w, pool_scale, pool_out, qkv_conv_w, a_log, dt_bias, dn_norm_w,
           dn_out, w_o, norm2_w, ffn_up, ffn_conv_w, ffn_conv_b, ffn_down, final_norm_w):
    B, S, D = x.shape
    depth = w_in.shape[0]
    pool_dim = pool_out.shape[1]
    dn = dn_out.shape[1]
    heads = a_log.shape[-1]
    F = ffn_down.shape[1]
    assert depth == 1, "the final rmsnorm is fused into the (single) layer's channel mixer"
    assert dn == heads * HEAD_DIM and CHUNK == HEAD_DIM == LANES
    assert S % (CHUNK * DELTA_BATCH) == 0 and S // CHUNK <= LANES
    assert S % IN_TILE == 0 and S % MERGE_TILE == 0 and S % FFN_TILE == 0 and F % FFN_CHUNK == 0
    nfc = F // FFN_CHUNK
    N = S // CHUNK
    row = lambda t: t.reshape(1, -1).astype(F32)

    for l in range(depth):
        wl = w_in[l].astype(BF16)
        c0 = pool_dim
        wp = wl[:, :c0]
        wq, wk, wv, wz = (wl[:, c0 + i * dn:c0 + (i + 1) * dn] for i in range(4))
        c1 = c0 + 4 * dn
        wgt = jnp.pad(wl[:, c1:c1 + 4 * heads], ((0, 0), (0, LANES - 4 * heads)))
        wg = wl[:, c1 + 4 * heads:]

        q, k, v, pm, z, g, gates = _in_proj(
            x, row(norm1_w[l]), wq, wk, wv, wp, wz, wg, wgt, qkv_conv_w[l].astype(F32),
            pool_w[l].astype(BF16), row(pool_scale[l]), heads)

        gates_rows = gates[:, :, :4 * heads].reshape(B, N, CHUNK, 4, heads).transpose(0, 3, 4, 1, 2)
        nw_cols = jnp.broadcast_to(dn_norm_w[l].astype(F32)[:, None], (HEAD_DIM, LANES))
        o = _delta(q, k, v, z, gates_rows, a_log[l].astype(F32), dt_bias[l].astype(F32), nw_cols)

        x = _merge(o, pm, g, x, dn_out[l].astype(BF16), pool_out[l].astype(BF16), w_o[l].astype(BF16))

        up = ffn_up[l].astype(BF16)
        chunked = lambda t: t.reshape(t.shape[0], nfc, FFN_CHUNK).transpose(1, 0, 2)
        cw = ffn_conv_w[l].astype(F32)
        cb = ffn_conv_b[l].astype(F32).reshape(1, -1)
        x = _ffn(x, row(norm2_w[l]), chunked(up[:, :F]), chunked(up[:, F:]), chunked(cw[:, :F]),
                 chunked(cw[:, F:]), chunked(cb[:, :F]), chunked(cb[:, F:]), ffn_down[l].astype(BF16),
                 row(final_norm_w))
    return x
```

```python
import functools

import jax
import jax.numpy as jnp
from jax import lax
from jax.experimental import pallas as pl
from jax.experimental.pallas import tpu as pltpu

F32 = jnp.float32
BF16 = jnp.bfloat16

NORM_EPS = 1e-6
L2_EPS = 1e-6
POOL_WINDOWS = (2, 4, 8, 16)
POOL_GROUP_DIM = 128
HEAD_DIM = 128
SHORT_CONV = 5
FFN_CONV = 3

SUBLANES = 8
LANES = 128
HALO = SUBLANES
CHUNK = 128
DELTA_BATCH = 8
VMEM_LIMIT = 56 * 1024 * 1024

IN_TILE = 512
MERGE_TILE = 1024
FFN_TILE = 512
FFN_CHUNK = 256


def _interleave(scr, val):
    n, width = val.shape
    pitch = n // SUBLANES
    blocks = width // LANES
    for c in range(blocks):
        scr[c] = val[:, c * LANES:(c + 1) * LANES]
    slabs = [jnp.concatenate([scr[c, pl.ds(a, SUBLANES, stride=pitch), :] for c in range(blocks)], axis=1)
             for a in range(pitch)]
    return jnp.concatenate(slabs, axis=0)


def _deinterleave(scr, val, lo, hi):
    n, width = val.shape
    pitch = n // SUBLANES
    blocks = width // LANES
    for a in range(pitch):
        for c in range(blocks):
            scr[c, pl.ds(a, SUBLANES, stride=pitch), :] = val[a * SUBLANES:(a + 1) * SUBLANES,
                                                             c * LANES:(c + 1) * LANES]
    return jnp.concatenate([scr[c, lo:hi, :] for c in range(blocks)], axis=1)


def _shift_interleaved(t, s):
    n = t.shape[0]
    if s == 0:
        return t
    m = abs(s)
    if s > 0:
        edge = [pltpu.roll(t[SUBLANES * j:SUBLANES * (j + 1)], SUBLANES - 1, axis=0) for j in range(m)]
        return jnp.concatenate([t[SUBLANES * m:]] + edge, axis=0)
    edge = [pltpu.roll(t[n - SUBLANES * j:n - SUBLANES * (j - 1)], 1, axis=0) for j in range(m, 0, -1)]
    return jnp.concatenate(edge + [t[:n - SUBLANES * m]], axis=0)


def _dot(a, b):
    return jnp.dot(a, b, preferred_element_type=F32)


def _rms(xv, w):
    ms = jnp.mean(xv * xv, axis=-1, keepdims=True)
    return xv * lax.rsqrt(ms + NORM_EPS) * w


def _halo_valid(n_rows, tile, j, nj):
    row = lax.broadcasted_iota(jnp.int32, (n_rows, 1), 0)
    top_ok = jnp.logical_or(row >= HALO, j > 0)
    bot_ok = jnp.logical_or(row < tile + HALO, j < nj - 1)
    return jnp.logical_and(top_ok, bot_ok)


def _halo_specs(tile, width, seq):
    per = tile // HALO
    last = seq // HALO - 1
    prev = pl.BlockSpec((1, HALO, width), lambda b, j: (b, jnp.maximum(j * per - 1, 0), 0))
    main = pl.BlockSpec((1, tile, width), lambda b, j: (b, j, 0))
    nxt = pl.BlockSpec((1, HALO, width), lambda b, j: (b, jnp.minimum((j + 1) * per, last), 0))
    return prev, main, nxt


def _const_spec(shape):
    nd = len(shape)
    return pl.BlockSpec(shape, lambda *_: (0,) * nd, pipeline_mode=pl.Buffered(1))


def _in_proj_kernel(xp_ref, xm_ref, xn_ref, n1w_ref, wq_ref, wk_ref, wv_ref, wp_ref, wz_ref,
                    wg_ref, wgt_ref, cw_ref, poolw_ref, pools_ref,
                    q_out, k_out, v_out, pm_out, z_out, g_out, gt_out, il_s, *, tile, seq, heads):
    j = pl.program_id(1)
    nj = pl.num_programs(1)
    n = tile + 2 * HALO
    dn = heads * HEAD_DIM
    xe = jnp.concatenate([xp_ref[0], xm_ref[0], xn_ref[0]], axis=0)
    h = _rms(xe, n1w_ref[...])
    h = jnp.where(_halo_valid(n, tile, j, nj), h, 0.0)
    hm = h[HALO:HALO + tile].astype(BF16)
    hi = _interleave(il_s, h).astype(BF16)

    def conv_silu(w_ref, col0):
        t = _dot(hi, w_ref[...])
        acc = None
        for tt in range(SHORT_CONV):
            term = cw_ref[tt:tt + 1, col0:col0 + dn] * _shift_interleaved(t, tt - SHORT_CONV // 2)
            acc = term if acc is None else acc + term
        return acc * jax.nn.sigmoid(acc)

    def l2n(y, scale):
        outs = []
        for hh in range(heads):
            th = y[:, hh * HEAD_DIM:(hh + 1) * HEAD_DIM]
            ss = jnp.sum(th * th, axis=-1, keepdims=True)
            outs.append(th * (lax.rsqrt(ss + L2_EPS) * scale))
        return jnp.concatenate(outs, axis=-1)

    def natural(y):
        return _deinterleave(il_s, y, HALO, HALO + tile).astype(BF16)

    q_out[0] = natural(l2n(conv_silu(wq_ref, 0), HEAD_DIM ** -0.5))
    k_out[0] = natural(l2n(conv_silu(wk_ref, dn), 1.0))
    v_out[0] = natural(conv_silu(wv_ref, 2 * dn))

    pe = _dot(hi, wp_ref[...])
    slot = lax.broadcasted_iota(jnp.int32, (n, 1), 0)
    pos = j * tile - HALO + (slot // SUBLANES) + (n // SUBLANES) * (slot % SUBLANES)
    pms = []
    for g, w in enumerate(POOL_WINDOWS):
        zg = pe[:, g * POOL_GROUP_DIM:(g + 1) * POOL_GROUP_DIM]
        f = zg
        span = 1
        while span < w:
            f = f + _shift_interleaved(f, span)
            span *= 2
        ws = _shift_interleaved(f, -(w // 2))
        start = jnp.maximum(pos - w // 2, 0)
        end = jnp.minimum(pos + w - w // 2, seq)
        cnt = jnp.maximum(end - start, 1).astype(F32)
        pooled = ws / cnt - zg
        pmg = _dot(pooled.astype(BF16), poolw_ref[g])
        pms.append(pmg * pools_ref[:, g * POOL_GROUP_DIM:(g + 1) * POOL_GROUP_DIM])
    pm_out[0] = natural(jnp.concatenate(pms, axis=1))

    z_out[0] = _dot(hm, wz_ref[...]).astype(BF16)
    g_out[0] = _dot(hm, wg_ref[...]).astype(BF16)
    gt_out[0] = _dot(hm, wgt_ref[...])


def _in_proj(x, n1w, wq, wk, wv, wp, wz, wg, wgt, cw, poolw, pools, heads):
    B, S, D = x.shape
    tile = IN_TILE
    dn = heads * HEAD_DIM
    pool_dim = wp.shape[1]
    xprev, xmain, xnext = _halo_specs(tile, D, S)
    out_spec = lambda width: pl.BlockSpec((1, tile, width), lambda b, j: (b, j, 0))
    return pl.pallas_call(
        functools.partial(_in_proj_kernel, tile=tile, seq=S, heads=heads),
        grid=(B, S // tile),
        in_specs=[xprev, xmain, xnext, _const_spec(n1w.shape), _const_spec(wq.shape),
                  _const_spec(wk.shape), _const_spec(wv.shape), _const_spec(wp.shape),
                  _const_spec(wz.shape), _const_spec(wg.shape), _const_spec(wgt.shape),
                  _const_spec(cw.shape), _const_spec(poolw.shape), _const_spec(pools.shape)],
        out_specs=[out_spec(dn), out_spec(dn), out_spec(dn), out_spec(pool_dim), out_spec(dn),
                   out_spec(wg.shape[1]), out_spec(LANES)],
        out_shape=[jax.ShapeDtypeStruct((B, S, dn), BF16)] * 3
        + [jax.ShapeDtypeStruct((B, S, pool_dim), BF16), jax.ShapeDtypeStruct((B, S, dn), BF16),
           jax.ShapeDtypeStruct((B, S, wg.shape[1]), BF16), jax.ShapeDtypeStruct((B, S, LANES), F32)],
        scratch_shapes=[pltpu.VMEM((max(D, dn) // LANES, tile + 2 * HALO, LANES), F32)],
        compiler_params=pltpu.CompilerParams(dimension_semantics=("parallel", "parallel"),
                                             vmem_limit_bytes=VMEM_LIMIT),
        name="in_proj",
    )(x, x, x, n1w, wq, wk, wv, wp, wz, wg, wgt, cw, poolw, pools)


def _softplus(t):
    return jnp.maximum(t, 0.0) + jnp.log1p(jnp.exp(-jnp.abs(t)))


def _scan_sum(a, axis, reverse):
    n = a.shape[axis]
    idx = lax.broadcasted_iota(jnp.int32, a.shape, axis)
    s = 1
    while s < n:
        if reverse:
            a = a + jnp.where(idx < n - s, pltpu.roll(a, n - s, axis=axis), 0.0)
        else:
            a = a + jnp.where(idx >= s, pltpu.roll(a, s, axis=axis), 0.0)
        s *= 2
    return a


def _bmm(a, b):
    return jnp.einsum("uik,ukj->uij", a, b, preferred_element_type=F32)


def _bmm_nt(a, b):
    return jnp.einsum("uik,ujk->uij", a, b, preferred_element_type=F32)


def _delta_kernel(alog_ref, dtb_ref, q_ref, k_ref, v_ref, z_ref, gates_ref, nwc_ref, o_ref,
                  grow_s, brow_s, rhs1_a, rhs2_a, ut_a, gl_a, rhs1_b, rhs2_b, ut_b, gl_b, o_s,
                  *, nchunks, batch):
    C = CHUNK
    N = nchunks
    U = batch
    D = HEAD_DIM
    stages = N // U
    hh = pl.program_id(1)

    for d in range(2):
        a = jnp.exp(jnp.full((1, 1), alog_ref[d, hh], F32))
        g_r = -a * _softplus(gates_ref[0, 2 + d, 0] + dtb_ref[d, hh])
        grow_s[d] = _scan_sum(g_r, 1, d == 1).reshape(N, 1, C)
        brow_s[d] = jax.nn.sigmoid(gates_ref[0, d, 0]).reshape(N, 1, C)

    rr = lax.broadcasted_iota(jnp.int32, (U, C, C), 1)
    cc = lax.broadcasted_iota(jnp.int32, (U, C, C), 2)

    def prep(g, bufs):
        rhs1_s, rhs2_s, ut_s, gl_s = bufs
        for d in range(2):
            n0 = g * U if d == 0 else N - (g + 1) * U
            r0 = n0 * C
            k = k_ref[0, pl.ds(r0, U * C), :].reshape(U, C, D)
            q = q_ref[0, pl.ds(r0, U * C), :].reshape(U, C, D)
            kf = k.astype(F32)
            kT = jnp.swapaxes(kf, 1, 2)
            qT = jnp.swapaxes(q.astype(F32), 1, 2)
            vT = jnp.swapaxes(v_ref[0, pl.ds(r0, U * C), :].reshape(U, C, D).astype(F32), 1, 2)
            kkq = _bmm_nt(k, jnp.concatenate([k, q], axis=1))
            KK = kkq[:, :, :C]
            QKt = kkq[:, :, C:]
            GR = grow_s[d, pl.ds(n0, U)]
            BR = brow_s[d, pl.ds(n0, U)]
            GC = jnp.swapaxes(jnp.broadcast_to(GR, (U, C, C)), 1, 2)
            incl = (cc >= rr) if d == 0 else (cc <= rr)
            strict = (cc > rr) if d == 0 else (cc < rr)
            decay = jnp.where(incl, jnp.exp(jnp.minimum(GR - GC, 0.0)), 0.0)
            Lt = jnp.where(strict, KK * BR * decay, 0.0)
            attn = QKt * decay
            T = jnp.where(rr == cc, 1.0, -Lt)
            Lb = Lt.astype(BF16)
            P = _bmm(Lb, Lb)
            rounds = C.bit_length() - 2
            for r in range(rounds):
                Pb = P.astype(BF16)
                if r < rounds - 1:
                    pt = _bmm(Pb, jnp.concatenate([T.astype(BF16), Pb], axis=2))
                    T = T + pt[:, :, :C]
                    P = pt[:, :, C:]
                else:
                    T = T + _bmm(Pb, T.astype(BF16))
            Y = jnp.where(rr == cc, 0.0, T)
            eG = jnp.exp(GR)
            X = jnp.concatenate([kT * (BR * eG), vT * BR], axis=1)
            wu = X + _bmm(X.astype(BF16), Y.astype(BF16))
            g_last = GC[:, C - 1:C, :] if d == 0 else GC[:, 0:1, :]
            rest = jnp.exp(g_last - GR)
            kd = kf * jnp.swapaxes(jnp.broadcast_to(rest, (U, C, C)), 1, 2)
            rhs1_s[d] = jnp.concatenate([wu[:, :D], qT * eG], axis=2).astype(BF16)
            rhs2_s[d] = jnp.concatenate([attn, kd], axis=2).astype(BF16)
            ut_s[d] = wu[:, D:]
            gl_s[d] = jnp.exp(g_last)

    def finish(n, ot):
        ms = jnp.mean(ot * ot, axis=0, keepdims=True)
        o = (ot * lax.rsqrt(ms + NORM_EPS) * nwc_ref[...]).T
        zz = z_ref[0, pl.ds(n * C, C), :].astype(F32)
        o_ref[0, pl.ds(n * C, C), :] = (o * (zz * jax.nn.sigmoid(zz))).astype(BF16)

    def serial(g, bufs, states):
        rhs1_s, rhs2_s, ut_s, gl_s = bufs
        for j in range(U):
            new_states = []
            for d in range(2):
                jd = j if d == 0 else U - 1 - j
                step = g * U + j
                n = step if d == 0 else N - 1 - step
                ST = states[d]
                p1 = _dot(ST.astype(BF16), rhs1_s[d, jd])
                vn = ut_s[d, jd] - p1[:, :C]
                p2 = _dot(vn.astype(BF16), rhs2_s[d, jd])
                ot = p1[:, C:] + p2[:, :C]
                if step < N // 2:
                    o_s[n] = ot
                else:
                    finish(n, ot + o_s[n])
                new_states.append(ST * gl_s[d, jd] + p2[:, C:])
            states = tuple(new_states)
        return states

    bufs = ((rhs1_a, rhs2_a, ut_a, gl_a), (rhs1_b, rhs2_b, ut_b, gl_b))
    zero = jnp.zeros((D, D), F32)
    states = (zero, zero)
    prep(0, bufs[0])
    for g in range(stages):
        if g + 1 < stages:
            prep(g + 1, bufs[(g + 1) % 2])
        states = serial(g, bufs[g % 2], states)


def _delta(q, k, v, z, gates_rows, a_log, dt_bias, nw_cols):
    B, S, dn = q.shape
    heads = dn // HEAD_DIM
    N = S // CHUNK
    U = DELTA_BATCH
    assert N % U == 0 and N % 2 == 0 and U % 2 == 0
    head_spec = pl.BlockSpec((1, S, HEAD_DIM), lambda b, h: (b, 0, h))
    smem = pl.BlockSpec(memory_space=pltpu.SMEM)
    return pl.pallas_call(
        functools.partial(_delta_kernel, nchunks=N, batch=DELTA_BATCH),
        grid=(B, heads),
        in_specs=[smem, smem, head_spec, head_spec, head_spec, head_spec,
                  pl.BlockSpec((1, 4, 1, N, CHUNK), lambda b, h: (b, 0, h, 0, 0)),
                  pl.BlockSpec((HEAD_DIM, LANES), lambda b, h: (0, 0))],
        out_specs=head_spec,
        out_shape=jax.ShapeDtypeStruct((B, S, dn), BF16),
        scratch_shapes=[
            pltpu.VMEM((2, N, 1, CHUNK), F32),
            pltpu.VMEM((2, N, 1, CHUNK), F32),
        ] + 2 * [
            pltpu.VMEM((2, U, HEAD_DIM, 2 * CHUNK), BF16),
            pltpu.VMEM((2, U, CHUNK, CHUNK + HEAD_DIM), BF16),
            pltpu.VMEM((2, U, HEAD_DIM, CHUNK), F32),
            pltpu.VMEM((2, U, 1, LANES), F32),
        ] + [pltpu.VMEM((N, HEAD_DIM, CHUNK), F32)],
        compiler_params=pltpu.CompilerParams(dimension_semantics=("parallel", "parallel"),
                                             vmem_limit_bytes=VMEM_LIMIT),
        name="delta_rule",
    )(a_log, dt_bias, q, k, v, z, gates_rows, nw_cols)


def _merge_kernel(o_ref, pm_ref, g_ref, x_ref, dn_ref, po_ref, wo_ref, out_ref):
    d = out_ref.shape[-1]
    y_dn = _dot(o_ref[0], dn_ref[...])
    y_pool = _dot(pm_ref[0], po_ref[...])
    g = g_ref[0].astype(F32)
    merged = jax.nn.sigmoid(g[:, :d]) * y_pool + jax.nn.sigmoid(g[:, d:]) * y_dn
    out_ref[0] = x_ref[0] + _dot(merged.astype(BF16), wo_ref[...])


def _merge(o, pm, g, x, dn_out, pool_out, w_o):
    B, S, D = x.shape
    tile = MERGE_TILE
    spec = lambda width: pl.BlockSpec((1, tile, width), lambda b, j: (b, j, 0))
    return pl.pallas_call(
        _merge_kernel,
        grid=(B, S // tile),
        in_specs=[spec(o.shape[-1]), spec(pm.shape[-1]), spec(g.shape[-1]), spec(D),
                  _const_spec(dn_out.shape), _const_spec(pool_out.shape), _const_spec(w_o.shape)],
        out_specs=spec(D),
        out_shape=jax.ShapeDtypeStruct((B, S, D), F32),
        compiler_params=pltpu.CompilerParams(dimension_semantics=("parallel", "parallel"),
                                             vmem_limit_bytes=VMEM_LIMIT),
        name="merge",
    )(o, pm, g, x, dn_out, pool_out, w_o)


def _ffn_kernel(xp_ref, xm_ref, xn_ref, n2w_ref, upg_ref, upv_ref, cwg_ref, cwv_ref, cbg_ref,
                cbv_ref, down_ref, fw_ref, out_ref, act_s, il_s, *, tile, nfc):
    j = pl.program_id(1)
    nj = pl.num_programs(1)
    n = tile + 2 * HALO
    xm = xm_ref[0]
    xe = jnp.concatenate([xp_ref[0], xm, xn_ref[0]], axis=0)
    h = jnp.where(_halo_valid(n, tile, j, nj), _rms(xe, n2w_ref[...]), 0.0)
    hb = _interleave(il_s, h).astype(BF16)

    def conv(u, cw_ref, cb_ref, c):
        acc = None
        for tt in range(FFN_CONV):
            term = cw_ref[c, tt:tt + 1, :] * _shift_interleaved(u, tt - FFN_CONV // 2)
            acc = term if acc is None else acc + term
        return acc + cb_ref[c]

    for c in range(nfc):
        gate = conv(_dot(hb, upg_ref[c]), cwg_ref, cbg_ref, c)
        val = conv(_dot(hb, upv_ref[c]), cwv_ref, cbv_ref, c)
        act_s[:, c * FFN_CHUNK:(c + 1) * FFN_CHUNK] = (gate * jax.nn.sigmoid(gate) * val).astype(BF16)

    y = _deinterleave(il_s, _dot(act_s[...], down_ref[...]), HALO, HALO + tile)
    out_ref[0] = _rms(xm + y, fw_ref[...])


def _ffn(x, n2w, upg, upv, cwg, cwv, cbg, cbv, down, fw):
    B, S, D = x.shape
    tile = FFN_TILE
    nfc = upg.shape[0]
    xprev, xmain, xnext = _halo_specs(tile, D, S)
    return pl.pallas_call(
        functools.partial(_ffn_kernel, tile=tile, nfc=nfc),
        grid=(B, S // tile),
        in_specs=[xprev, xmain, xnext, _const_spec(n2w.shape), _const_spec(upg.shape),
                  _const_spec(upv.shape), _const_spec(cwg.shape), _const_spec(cwv.shape),
                  _const_spec(cbg.shape), _const_spec(cbv.shape), _const_spec(down.shape),
                  _const_spec(fw.shape)],
        out_specs=pl.BlockSpec((1, tile, D), lambda b, j: (b, j, 0)),
        out_shape=jax.ShapeDtypeStruct((B, S, D), F32),
        scratch_shapes=[pltpu.VMEM((tile + 2 * HALO, nfc * FFN_CHUNK), BF16),
                        pltpu.VMEM((D // LANES, tile + 2 * HALO, LANES), F32)],
        compiler_params=pltpu.CompilerParams(dimension_semantics=("parallel", "parallel"),
                                             vmem_limit_bytes=VMEM_LIMIT),
        name="ffn",
    )(x, x, x, n2w, upg, upv, cwg, cwv, cbg, cbv, down, fw)


def kernel(x, norm1_w, w_in, pool_w, pool_scale, pool_out, qkv_conv_w, a_log, dt_bias, dn_norm_w,
           dn_out, w_o, norm2_w, ffn_up, ffn_conv_w, ffn_conv_b, ffn_down, final_norm_w):
    B, S, D = x.shape
    depth = w_in.shape[0]
    pool_dim = pool_out.shape[1]
    dn = dn_out.shape[1]
    heads = a_log.shape[-1]
    F = ffn_down.shape[1]
    assert depth == 1, "the final rmsnorm is fused into the (single) layer's channel mixer"
    assert dn == heads * HEAD_DIM and CHUNK == HEAD_DIM == LANES
    assert S % (CHUNK * DELTA_BATCH) == 0 and S // CHUNK <= LANES
    assert S % IN_TILE == 0 and S % MERGE_TILE == 0 and S % FFN_TILE == 0 and F % FFN_CHUNK == 0
    nfc = F // FFN_CHUNK
    N = S // CHUNK
    row = lambda t: t.reshape(1, -1).astype(F32)

    for l in range(depth):
        wl = w_in[l].astype(BF16)
        c0 = pool_dim
        wp = wl[:, :c0]
        wq, wk, wv, wz = (wl[:, c0 + i * dn:c0 + (i + 1) * dn] for i in range(4))
        c1 = c0 + 4 * dn
        wgt = jnp.pad(wl[:, c1:c1 + 4 * heads], ((0, 0), (0, LANES - 4 * heads)))
        wg = wl[:, c1 + 4 * heads:]

        q, k, v, pm, z, g, gates = _in_proj(
            x, row(norm1_w[l]), wq, wk, wv, wp, wz, wg, wgt, qkv_conv_w[l].astype(F32),
            pool_w[l].astype(BF16), row(pool_scale[l]), heads)

        gates_rows = gates[:, :, :4 * heads].reshape(B, N, CHUNK, 4, heads).transpose(0, 3, 4, 1, 2)
        nw_cols = jnp.broadcast_to(dn_norm_w[l].astype(F32)[:, None], (HEAD_DIM, LANES))
        o = _delta(q, k, v, z, gates_rows, a_log[l].astype(F32), dt_bias[l].astype(F32), nw_cols)

        x = _merge(o, pm, g, x, dn_out[l].astype(BF16), pool_out[l].astype(BF16), w_o[l].astype(BF16))

        up = ffn_up[l].astype(BF16)
        chunked = lambda t: t.reshape(t.shape[0], nfc, FFN_CHUNK).transpose(1, 0, 2)
        cw = ffn_conv_w[l].astype(F32)
        cb = ffn_conv_b[l].astype(F32).reshape(1, -1)
        x = _ffn(x, row(norm2_w[l]), chunked(up[:, :F]), chunked(up[:, F:]), chunked(cw[:, :F]),
                 chunked(cw[:, F:]), chunked(cb[:, :F]), chunked(cb[:, F:]), ffn_down[l].astype(BF16),
                 row(final_norm_w))
    return x
```

```python
import functools

import jax
import jax.numpy as jnp
from jax import lax
from jax.experimental import pallas as pl
from jax.experimental.pallas import tpu as pltpu

F32 = jnp.float32
BF16 = jnp.bfloat16

NORM_EPS = 1e-6
L2_EPS = 1e-6
POOL_WINDOWS = (2, 4, 8, 16)
POOL_GROUP_DIM = 128
HEAD_DIM = 128
SHORT_CONV = 5
FFN_CONV = 3

SUBLANES = 8
LANES = 128
HALO = SUBLANES
CHUNK = 128
DELTA_BATCH = 8
DELTA_HEADS = 2
VMEM_LIMIT = 56 * 1024 * 1024

IN_TILE = 512
MERGE_TILE = 1024
FFN_TILE = 512
FFN_CHUNK = 256


def _interleave(scr, val):
    n, width = val.shape
    pitch = n // SUBLANES
    blocks = width // LANES
    for c in range(blocks):
        scr[c] = val[:, c * LANES:(c + 1) * LANES]
    slabs = [jnp.concatenate([scr[c, pl.ds(a, SUBLANES, stride=pitch), :] for c in range(blocks)], axis=1)
             for a in range(pitch)]
    return jnp.concatenate(slabs, axis=0)


def _deinterleave(scr, val, lo, hi):
    n, width = val.shape
    pitch = n // SUBLANES
    blocks = width // LANES
    for a in range(pitch):
        for c in range(blocks):
            scr[c, pl.ds(a, SUBLANES, stride=pitch), :] = val[a * SUBLANES:(a + 1) * SUBLANES,
                                                             c * LANES:(c + 1) * LANES]
    return jnp.concatenate([scr[c, lo:hi, :] for c in range(blocks)], axis=1)


def _shift_interleaved(t, s):
    n = t.shape[0]
    if s == 0:
        return t
    m = abs(s)
    if s > 0:
        edge = [pltpu.roll(t[SUBLANES * j:SUBLANES * (j + 1)], SUBLANES - 1, axis=0) for j in range(m)]
        return jnp.concatenate([t[SUBLANES * m:]] + edge, axis=0)
    edge = [pltpu.roll(t[n - SUBLANES * j:n - SUBLANES * (j - 1)], 1, axis=0) for j in range(m, 0, -1)]
    return jnp.concatenate(edge + [t[:n - SUBLANES * m]], axis=0)


def _dot(a, b):
    return jnp.dot(a, b, preferred_element_type=F32)


def _rms(xv, w):
    ms = jnp.mean(xv * xv, axis=-1, keepdims=True)
    return xv * lax.rsqrt(ms + NORM_EPS) * w


def _halo_valid(n_rows, tile, j, nj):
    row = lax.broadcasted_iota(jnp.int32, (n_rows, 1), 0)
    top_ok = jnp.logical_or(row >= HALO, j > 0)
    bot_ok = jnp.logical_or(row < tile + HALO, j < nj - 1)
    return jnp.logical_and(top_ok, bot_ok)


def _halo_specs(tile, width, seq):
    per = tile // HALO
    last = seq // HALO - 1
    prev = pl.BlockSpec((1, HALO, width), lambda b, j: (b, jnp.maximum(j * per - 1, 0), 0))
    main = pl.BlockSpec((1, tile, width), lambda b, j: (b, j, 0))
    nxt = pl.BlockSpec((1, HALO, width), lambda b, j: (b, jnp.minimum((j + 1) * per, last), 0))
    return prev, main, nxt


def _const_spec(shape):
    nd = len(shape)
    return pl.BlockSpec(shape, lambda *_: (0,) * nd, pipeline_mode=pl.Buffered(1))


def _in_proj_kernel(xp_ref, xm_ref, xn_ref, n1w_ref, wq_ref, wk_ref, wv_ref, wp_ref, wz_ref,
                    wg_ref, wgt_ref, cw_ref, poolw_ref, pools_ref,
                    q_out, k_out, v_out, pm_out, z_out, g_out, gt_out, il_s, *, tile, seq, heads):
    j = pl.program_id(1)
    nj = pl.num_programs(1)
    n = tile + 2 * HALO
    dn = heads * HEAD_DIM
    xe = jnp.concatenate([xp_ref[0], xm_ref[0], xn_ref[0]], axis=0)
    h = _rms(xe, n1w_ref[...])
    h = jnp.where(_halo_valid(n, tile, j, nj), h, 0.0)
    hm = h[HALO:HALO + tile].astype(BF16)
    hi = _interleave(il_s, h).astype(BF16)

    def conv_silu(w_ref, col0):
        t = _dot(hi, w_ref[...])
        acc = None
        for tt in range(SHORT_CONV):
            term = cw_ref[tt:tt + 1, col0:col0 + dn] * _shift_interleaved(t, tt - SHORT_CONV // 2)
            acc = term if acc is None else acc + term
        return acc * jax.nn.sigmoid(acc)

    def l2n(y, scale):
        outs = []
        for hh in range(heads):
            th = y[:, hh * HEAD_DIM:(hh + 1) * HEAD_DIM]
            ss = jnp.sum(th * th, axis=-1, keepdims=True)
            outs.append(th * (lax.rsqrt(ss + L2_EPS) * scale))
        return jnp.concatenate(outs, axis=-1)

    def natural(y):
        return _deinterleave(il_s, y, HALO, HALO + tile).astype(BF16)

    q_out[0] = natural(l2n(conv_silu(wq_ref, 0), HEAD_DIM ** -0.5))
    k_out[0] = natural(l2n(conv_silu(wk_ref, dn), 1.0))
    v_out[0] = natural(conv_silu(wv_ref, 2 * dn))

    pe = _dot(hi, wp_ref[...])
    slot = lax.broadcasted_iota(jnp.int32, (n, 1), 0)
    pos = j * tile - HALO + (slot // SUBLANES) + (n // SUBLANES) * (slot % SUBLANES)
    pms = []
    for g, w in enumerate(POOL_WINDOWS):
        zg = pe[:, g * POOL_GROUP_DIM:(g + 1) * POOL_GROUP_DIM]
        f = zg
        span = 1
        while span < w:
            f = f + _shift_interleaved(f, span)
            span *= 2
        ws = _shift_interleaved(f, -(w // 2))
        start = jnp.maximum(pos - w // 2, 0)
        end = jnp.minimum(pos + w - w // 2, seq)
        cnt = jnp.maximum(end - start, 1).astype(F32)
        pooled = ws / cnt - zg
        pmg = _dot(pooled.astype(BF16), poolw_ref[g])
        pms.append(pmg * pools_ref[:, g * POOL_GROUP_DIM:(g + 1) * POOL_GROUP_DIM])
    pm_out[0] = natural(jnp.concatenate(pms, axis=1))

    z_out[0] = _dot(hm, wz_ref[...]).astype(BF16)
    g_out[0] = _dot(hm, wg_ref[...]).astype(BF16)
    gt = _dot(hm, wgt_ref[...])
    gt_t = jnp.concatenate([gt[r:r + LANES].T for r in range(0, tile, LANES)], axis=1)
    gt_out[0] = gt_t[:gt_out.shape[1]]


def _in_proj(x, n1w, wq, wk, wv, wp, wz, wg, wgt, cw, poolw, pools, heads):
    B, S, D = x.shape
    tile = IN_TILE
    dn = heads * HEAD_DIM
    pool_dim = wp.shape[1]
    xprev, xmain, xnext = _halo_specs(tile, D, S)
    out_spec = lambda width: pl.BlockSpec((1, tile, width), lambda b, j: (b, j, 0))
    return pl.pallas_call(
        functools.partial(_in_proj_kernel, tile=tile, seq=S, heads=heads),
        grid=(B, S // tile),
        in_specs=[xprev, xmain, xnext, _const_spec(n1w.shape), _const_spec(wq.shape),
                  _const_spec(wk.shape), _const_spec(wv.shape), _const_spec(wp.shape),
                  _const_spec(wz.shape), _const_spec(wg.shape), _const_spec(wgt.shape),
                  _const_spec(cw.shape), _const_spec(poolw.shape), _const_spec(pools.shape)],
        out_specs=[out_spec(dn), out_spec(dn), out_spec(dn), out_spec(pool_dim), out_spec(dn),
                   out_spec(wg.shape[1]), pl.BlockSpec((1, 4 * heads, tile), lambda b, j: (b, 0, j))],
        out_shape=[jax.ShapeDtypeStruct((B, S, dn), BF16)] * 3
        + [jax.ShapeDtypeStruct((B, S, pool_dim), BF16), jax.ShapeDtypeStruct((B, S, dn), BF16),
           jax.ShapeDtypeStruct((B, S, wg.shape[1]), BF16),
           jax.ShapeDtypeStruct((B, 4 * heads, S), F32)],
        scratch_shapes=[pltpu.VMEM((max(D, dn) // LANES, tile + 2 * HALO, LANES), F32)],
        compiler_params=pltpu.CompilerParams(dimension_semantics=("parallel", "parallel"),
                                             vmem_limit_bytes=VMEM_LIMIT),
        name="in_proj",
    )(x, x, x, n1w, wq, wk, wv, wp, wz, wg, wgt, cw, poolw, pools)


def _softplus(t):
    return jnp.maximum(t, 0.0) + jnp.log1p(jnp.exp(-jnp.abs(t)))


def _scan_sum(a, axis, reverse):
    n = a.shape[axis]
    idx = lax.broadcasted_iota(jnp.int32, a.shape, axis)
    s = 1
    while s < n:
        if reverse:
            a = a + jnp.where(idx < n - s, pltpu.roll(a, n - s, axis=axis), 0.0)
        else:
            a = a + jnp.where(idx >= s, pltpu.roll(a, s, axis=axis), 0.0)
        s *= 2
    return a


def _bmm(a, b):
    return jnp.einsum("uik,ukj->uij", a, b, preferred_element_type=F32)


def _bmm_nt(a, b):
    return jnp.einsum("uik,ujk->uij", a, b, preferred_element_type=F32)


def _delta_kernel(alog_ref, dtb_ref, q_ref, k_ref, v_ref, z_ref, gates_ref, nwc_ref, o_ref,
                  grow_s, brow_s, rhs1_a, rhs2_a, ut_a, gl_a, rhs1_b, rhs2_b, ut_b, gl_b, o_s,
                  *, nchunks, batch):
    C = CHUNK
    N = nchunks
    U = batch
    D = HEAD_DIM
    stages = N // U
    heads_here = q_ref.shape[-1] // D

    for hq in range(heads_here):
        hh = pl.program_id(1) * heads_here + hq
        for d in range(2):
            a = jnp.exp(jnp.full((1, 1), alog_ref[d, hh], F32))
            g_r = -a * _softplus(gates_ref[0, 2 + d, hq] + dtb_ref[d, hh])
            grow_s[hq, d] = _scan_sum(g_r, 1, d == 1).reshape(N, 1, C)
            brow_s[hq, d] = jax.nn.sigmoid(gates_ref[0, d, hq]).reshape(N, 1, C)

    rr = lax.broadcasted_iota(jnp.int32, (U, C, C), 1)
    cc = lax.broadcasted_iota(jnp.int32, (U, C, C), 2)

    def prep(hq, g, bufs):
        rhs1_s, rhs2_s, ut_s, gl_s = bufs
        cols = pl.ds(hq * D, D)
        for d in range(2):
            n0 = g * U if d == 0 else N - (g + 1) * U
            r0 = n0 * C
            k = k_ref[0, pl.ds(r0, U * C), cols].reshape(U, C, D)
            q = q_ref[0, pl.ds(r0, U * C), cols].reshape(U, C, D)
            kf = k.astype(F32)
            kT = jnp.swapaxes(kf, 1, 2)
            qT = jnp.swapaxes(q.astype(F32), 1, 2)
            vT = jnp.swapaxes(v_ref[0, pl.ds(r0, U * C), cols].reshape(U, C, D).astype(F32), 1, 2)
            kkq = _bmm_nt(k, jnp.concatenate([k, q], axis=1))
            KK = kkq[:, :, :C]
            QKt = kkq[:, :, C:]
            GR = grow_s[hq, d, pl.ds(n0, U)]
            BR = brow_s[hq, d, pl.ds(n0, U)]
            GC = jnp.swapaxes(jnp.broadcast_to(GR, (U, C, C)), 1, 2)
            incl = (cc >= rr) if d == 0 else (cc <= rr)
            strict = (cc > rr) if d == 0 else (cc < rr)
            decay = jnp.where(incl, jnp.exp(jnp.minimum(GR - GC, 0.0)), 0.0)
            Lt = jnp.where(strict, KK * BR * decay, 0.0)
            attn = QKt * decay
            Y = -Lt
            Lb = Lt.astype(BF16)
            P = _bmm(Lb, Lb)
            rounds = C.bit_length() - 2
            for r in range(rounds):
                Pb = P.astype(BF16)
                if r < rounds - 1:
                    py = _bmm(Pb, jnp.concatenate([Y.astype(BF16), Pb], axis=2))
                    Y = Y + P + py[:, :, :C]
                    P = py[:, :, C:]
                else:
                    Y = Y + P + _bmm(Pb, Y.astype(BF16))
            eG = jnp.exp(GR)
            X = jnp.concatenate([kT * (BR * eG), vT * BR], axis=1)
            wu = X + _bmm(X.astype(BF16), Y.astype(BF16))
            g_last = GC[:, C - 1:C, :] if d == 0 else GC[:, 0:1, :]
            rest = jnp.exp(g_last - GR)
            kd = kf * jnp.swapaxes(jnp.broadcast_to(rest, (U, C, C)), 1, 2)
            rhs1_s[d] = jnp.concatenate([wu[:, :D], qT * eG], axis=2).astype(BF16)
            rhs2_s[d] = jnp.concatenate([attn, kd], axis=2).astype(BF16)
            ut_s[d] = wu[:, D:]
            gl_s[d] = jnp.exp(g_last)

    def finish(hq, n, ot):
        ms = jnp.mean(ot * ot, axis=0, keepdims=True)
        o = (ot * lax.rsqrt(ms + NORM_EPS) * nwc_ref[...]).T
        zz = z_ref[0, pl.ds(n * C, C), pl.ds(hq * D, D)].astype(F32)
        o_ref[0, pl.ds(n * C, C), pl.ds(hq * D, D)] = (o * (zz * jax.nn.sigmoid(zz))).astype(BF16)

    def serial(hq, g, bufs, states):
        rhs1_s, rhs2_s, ut_s, gl_s = bufs
        for j in range(U):
            new_states = []
            for d in range(2):
                jd = j if d == 0 else U - 1 - j
                step = g * U + j
                n = step if d == 0 else N - 1 - step
                ST = states[d]
                p1 = _dot(ST.astype(BF16), rhs1_s[d, jd])
                vn = ut_s[d, jd] - p1[:, :C]
                p2 = _dot(vn.astype(BF16), rhs2_s[d, jd])
                ot = p1[:, C:] + p2[:, :C]
                if step < N // 2:
                    o_s[n] = ot
                else:
                    finish(hq, n, ot + o_s[n])
                new_states.append(ST * gl_s[d, jd] + p2[:, C:])
            states = tuple(new_states)
        return states

    bufs = ((rhs1_a, rhs2_a, ut_a, gl_a), (rhs1_b, rhs2_b, ut_b, gl_b))
    zero = jnp.zeros((D, D), F32)
    work = [(hq, g) for hq in range(heads_here) for g in range(stages)]
    prep(*work[0], bufs[0])
    for i, (hq, g) in enumerate(work):
        if i + 1 < len(work):
            prep(*work[i + 1], bufs[(i + 1) % 2])
        states = serial(hq, g, bufs[i % 2], (zero, zero) if g == 0 else states)


def _delta(q, k, v, z, gates_rows, a_log, dt_bias, nw_cols):
    B, S, dn = q.shape
    heads = dn // HEAD_DIM
    N = S // CHUNK
    U = DELTA_BATCH
    assert N % U == 0 and N % 2 == 0 and U % 2 == 0
    hp = DELTA_HEADS
    assert heads % hp == 0
    head_spec = pl.BlockSpec((1, S, hp * HEAD_DIM), lambda b, h: (b, 0, h))
    smem = pl.BlockSpec(memory_space=pltpu.SMEM)
    return pl.pallas_call(
        functools.partial(_delta_kernel, nchunks=N, batch=DELTA_BATCH),
        grid=(B, heads // hp),
        in_specs=[smem, smem, head_spec, head_spec, head_spec, head_spec,
                  pl.BlockSpec((1, 4, hp, N, CHUNK), lambda b, h: (b, 0, h, 0, 0)),
                  pl.BlockSpec((HEAD_DIM, LANES), lambda b, h: (0, 0))],
        out_specs=head_spec,
        out_shape=jax.ShapeDtypeStruct((B, S, dn), BF16),
        scratch_shapes=[
            pltpu.VMEM((hp, 2, N, 1, CHUNK), F32),
            pltpu.VMEM((hp, 2, N, 1, CHUNK), F32),
        ] + 2 * [
            pltpu.VMEM((2, U, HEAD_DIM, 2 * CHUNK), BF16),
            pltpu.VMEM((2, U, CHUNK, CHUNK + HEAD_DIM), BF16),
            pltpu.VMEM((2, U, HEAD_DIM, CHUNK), F32),
            pltpu.VMEM((2, U, 1, LANES), F32),
        ] + [pltpu.VMEM((N, HEAD_DIM, CHUNK), F32)],
        compiler_params=pltpu.CompilerParams(dimension_semantics=("parallel", "parallel"),
                                             vmem_limit_bytes=VMEM_LIMIT),
        name="delta_rule",
    )(a_log, dt_bias, q, k, v, z, gates_rows, nw_cols)


def _merge_kernel(o_ref, pm_ref, g_ref, x_ref, dn_ref, po_ref, wo_ref, out_ref):
    d = out_ref.shape[-1]
    y_dn = _dot(o_ref[0], dn_ref[...])
    y_pool = _dot(pm_ref[0], po_ref[...])
    g = g_ref[0].astype(F32)
    merged = jax.nn.sigmoid(g[:, :d]) * y_pool + jax.nn.sigmoid(g[:, d:]) * y_dn
    out_ref[0] = x_ref[0] + _dot(merged.astype(BF16), wo_ref[...])


def _merge(o, pm, g, x, dn_out, pool_out, w_o):
    B, S, D = x.shape
    tile = MERGE_TILE
    spec = lambda width: pl.BlockSpec((1, tile, width), lambda b, j: (b, j, 0))
    return pl.pallas_call(
        _merge_kernel,
        grid=(B, S // tile),
        in_specs=[spec(o.shape[-1]), spec(pm.shape[-1]), spec(g.shape[-1]), spec(D),
                  _const_spec(dn_out.shape), _const_spec(pool_out.shape), _const_spec(w_o.shape)],
        out_specs=spec(D),
        out_shape=jax.ShapeDtypeStruct((B, S, D), F32),
        compiler_params=pltpu.CompilerParams(dimension_semantics=("parallel", "parallel"),
                                             vmem_limit_bytes=VMEM_LIMIT),
        name="merge",
    )(o, pm, g, x, dn_out, pool_out, w_o)


def _ffn_kernel(xp_ref, xm_ref, xn_ref, n2w_ref, up_ref, cw_ref, cb_ref, down_ref, fw_ref, out_ref,
                act_s, il_s, *, tile):
    j = pl.program_id(1)
    nj = pl.num_programs(1)
    n = tile + 2 * HALO
    hidden = down_ref.shape[0]
    xm = xm_ref[0]
    xe = jnp.concatenate([xp_ref[0], xm, xn_ref[0]], axis=0)
    h = jnp.where(_halo_valid(n, tile, j, nj), _rms(xe, n2w_ref[...]), 0.0)
    hb = _interleave(il_s, h).astype(BF16)

    def conv_up(col):
        cols = slice(col, col + FFN_CHUNK)
        u = _dot(hb, up_ref[:, cols])
        acc = None
        for tt in range(FFN_CONV):
            term = cw_ref[tt:tt + 1, cols] * _shift_interleaved(u, tt - FFN_CONV // 2)
            acc = term if acc is None else acc + term
        return acc + cb_ref[:, cols]

    for col in range(0, hidden, FFN_CHUNK):
        gate = conv_up(col)
        val = conv_up(hidden + col)
        act_s[:, col:col + FFN_CHUNK] = (gate * jax.nn.sigmoid(gate) * val).astype(BF16)

    y = _deinterleave(il_s, _dot(act_s[...], down_ref[...]), HALO, HALO + tile)
    out_ref[0] = _rms(xm + y, fw_ref[...])


def _ffn(x, n2w, up, cw, cb, down, fw):
    B, S, D = x.shape
    tile = FFN_TILE
    hidden = down.shape[0]
    xprev, xmain, xnext = _halo_specs(tile, D, S)
    return pl.pallas_call(
        functools.partial(_ffn_kernel, tile=tile),
        grid=(B, S // tile),
        in_specs=[xprev, xmain, xnext, _const_spec(n2w.shape), _const_spec(up.shape),
                  _const_spec(cw.shape), _const_spec(cb.shape), _const_spec(down.shape),
                  _const_spec(fw.shape)],
        out_specs=pl.BlockSpec((1, tile, D), lambda b, j: (b, j, 0)),
        out_shape=jax.ShapeDtypeStruct((B, S, D), F32),
        scratch_shapes=[pltpu.VMEM((tile + 2 * HALO, hidden), BF16),
                        pltpu.VMEM((D // LANES, tile + 2 * HALO, LANES), F32)],
        compiler_params=pltpu.CompilerParams(dimension_semantics=("parallel", "parallel"),
                                             vmem_limit_bytes=VMEM_LIMIT),
        name="ffn",
    )(x, x, x, n2w, up, cw, cb, down, fw)


def kernel(x, norm1_w, w_in, pool_w, pool_scale, pool_out, qkv_conv_w, a_log, dt_bias, dn_norm_w,
           dn_out, w_o, norm2_w, ffn_up, ffn_conv_w, ffn_conv_b, ffn_down, final_norm_w):
    B, S, D = x.shape
    depth = w_in.shape[0]
    pool_dim = pool_out.shape[1]
    dn = dn_out.shape[1]
    heads = a_log.shape[-1]
    F = ffn_down.shape[1]
    assert depth == 1, "the final rmsnorm is fused into the (single) layer's channel mixer"
    assert dn == heads * HEAD_DIM and CHUNK == HEAD_DIM == LANES
    assert S % (CHUNK * DELTA_BATCH) == 0 and S // CHUNK <= LANES
    assert S % IN_TILE == 0 and S % MERGE_TILE == 0 and S % FFN_TILE == 0 and F % FFN_CHUNK == 0
    N = S // CHUNK
    row = lambda t: t.reshape(1, -1).astype(F32)

    for l in range(depth):
        wl = w_in[l].astype(BF16)
        c0 = pool_dim
        wp = wl[:, :c0]
        wq, wk, wv, wz = (wl[:, c0 + i * dn:c0 + (i + 1) * dn] for i in range(4))
        c1 = c0 + 4 * dn
        wgt = jnp.pad(wl[:, c1:c1 + 4 * heads], ((0, 0), (0, LANES - 4 * heads)))
        wg = wl[:, c1 + 4 * heads:]

        q, k, v, pm, z, g, gates = _in_proj(
            x, row(norm1_w[l]), wq, wk, wv, wp, wz, wg, wgt, qkv_conv_w[l].astype(F32),
            pool_w[l].astype(BF16), row(pool_scale[l]), heads)

        gates_rows = gates.reshape(B, 4, heads, N, CHUNK)
        nw_cols = jnp.broadcast_to(dn_norm_w[l].astype(F32)[:, None], (HEAD_DIM, LANES))
        o = _delta(q, k, v, z, gates_rows, a_log[l].astype(F32), dt_bias[l].astype(F32), nw_cols)

        x = _merge(o, pm, g, x, dn_out[l].astype(BF16), pool_out[l].astype(BF16), w_o[l].astype(BF16))

        x = _ffn(x, row(norm2_w[l]), ffn_up[l].astype(BF16), ffn_conv_w[l].astype(F32),
                 row(ffn_conv_b[l]), ffn_down[l].astype(BF16), row(final_norm_w))
    return x
```

```python
import functools

import jax
import jax.numpy as jnp
from jax import lax
from jax.experimental import pallas as pl
from jax.experimental.pallas import tpu as pltpu

F32 = jnp.float32
BF16 = jnp.bfloat16

NORM_EPS = 1e-6
L2_EPS = 1e-6
POOL_WINDOWS = (2, 4, 8, 16)
POOL_GROUP_DIM = 128
HEAD_DIM = 128
SHORT_CONV = 5
FFN_CONV = 3

SUBLANES = 8
LANES = 128
HALO = SUBLANES
CHUNK = 128
DELTA_BATCH = 8
DELTA_HEADS = 2
VMEM_LIMIT = 56 * 1024 * 1024

IN_TILE = 512
MERGE_TILE = 1024
FFN_TILE = 512
FFN_CHUNK = 256


def _interleave(scr, val):
    n, width = val.shape
    pitch = n // SUBLANES
    blocks = width // LANES
    for c in range(blocks):
        scr[c] = val[:, c * LANES:(c + 1) * LANES]
    slabs = [jnp.concatenate([scr[c, pl.ds(a, SUBLANES, stride=pitch), :] for c in range(blocks)], axis=1)
             for a in range(pitch)]
    return jnp.concatenate(slabs, axis=0)


def _deinterleave(scr, val, lo, hi):
    n, width = val.shape
    pitch = n // SUBLANES
    blocks = width // LANES
    for a in range(pitch):
        for c in range(blocks):
            scr[c, pl.ds(a, SUBLANES, stride=pitch), :] = val[a * SUBLANES:(a + 1) * SUBLANES,
                                                             c * LANES:(c + 1) * LANES]
    return jnp.concatenate([scr[c, lo:hi, :] for c in range(blocks)], axis=1)


def _shift_interleaved(t, s):
    n = t.shape[0]
    if s == 0:
        return t
    m = abs(s)
    if s > 0:
        edge = [pltpu.roll(t[SUBLANES * j:SUBLANES * (j + 1)], SUBLANES - 1, axis=0) for j in range(m)]
        return jnp.concatenate([t[SUBLANES * m:]] + edge, axis=0)
    edge = [pltpu.roll(t[n - SUBLANES * j:n - SUBLANES * (j - 1)], 1, axis=0) for j in range(m, 0, -1)]
    return jnp.concatenate(edge + [t[:n - SUBLANES * m]], axis=0)


def _dot(a, b):
    return jnp.dot(a, b, preferred_element_type=F32)


def _rms(xv, w):
    ms = jnp.mean(xv * xv, axis=-1, keepdims=True)
    return xv * lax.rsqrt(ms + NORM_EPS) * w


def _halo_valid(n_rows, tile, j, nj):
    row = lax.broadcasted_iota(jnp.int32, (n_rows, 1), 0)
    top_ok = jnp.logical_or(row >= HALO, j > 0)
    bot_ok = jnp.logical_or(row < tile + HALO, j < nj - 1)
    return jnp.logical_and(top_ok, bot_ok)


def _halo_specs(tile, width, seq):
    per = tile // HALO
    last = seq // HALO - 1
    prev = pl.BlockSpec((1, HALO, width), lambda b, j: (b, jnp.maximum(j * per - 1, 0), 0))
    main = pl.BlockSpec((1, tile, width), lambda b, j: (b, j, 0))
    nxt = pl.BlockSpec((1, HALO, width), lambda b, j: (b, jnp.minimum((j + 1) * per, last), 0))
    return prev, main, nxt


def _const_spec(shape):
    nd = len(shape)
    return pl.BlockSpec(shape, lambda *_: (0,) * nd, pipeline_mode=pl.Buffered(1))


def _in_proj_kernel(xp_ref, xm_ref, xn_ref, n1w_ref, wq_ref, wk_ref, wv_ref, wp_ref, wz_ref,
                    wg_ref, wgt_ref, cw_ref, poolw_ref, pools_ref,
                    q_out, k_out, v_out, pm_out, z_out, g_out, gt_out, il_s, *, tile, seq, heads):
    j = pl.program_id(1)
    nj = pl.num_programs(1)
    n = tile + 2 * HALO
    dn = heads * HEAD_DIM
    xe = jnp.concatenate([xp_ref[0], xm_ref[0], xn_ref[0]], axis=0)
    h = _rms(xe, n1w_ref[...])
    h = jnp.where(_halo_valid(n, tile, j, nj), h, 0.0)
    hm = h[HALO:HALO + tile].astype(BF16)
    hi = _interleave(il_s, h).astype(BF16)

    def conv_silu(w_ref, col0):
        t = _dot(hi, w_ref[...])
        acc = None
        for tt in range(SHORT_CONV):
            term = cw_ref[tt:tt + 1, col0:col0 + dn] * _shift_interleaved(t, tt - SHORT_CONV // 2)
            acc = term if acc is None else acc + term
        return acc * jax.nn.sigmoid(acc)

    def l2n(y, scale):
        outs = []
        for hh in range(heads):
            th = y[:, hh * HEAD_DIM:(hh + 1) * HEAD_DIM]
            ss = jnp.sum(th * th, axis=-1, keepdims=True)
            outs.append(th * (lax.rsqrt(ss + L2_EPS) * scale))
        return jnp.concatenate(outs, axis=-1)

    def natural(y):
        return _deinterleave(il_s, y, HALO, HALO + tile).astype(BF16)

    q_out[0] = natural(l2n(conv_silu(wq_ref, 0), HEAD_DIM ** -0.5))
    k_out[0] = natural(l2n(conv_silu(wk_ref, dn), 1.0))
    v_out[0] = natural(conv_silu(wv_ref, 2 * dn))

    pe = _dot(hi, wp_ref[...])
    slot = lax.broadcasted_iota(jnp.int32, (n, 1), 0)
    pos = j * tile - HALO + (slot // SUBLANES) + (n // SUBLANES) * (slot % SUBLANES)
    pms = []
    for g, w in enumerate(POOL_WINDOWS):
        zg = pe[:, g * POOL_GROUP_DIM:(g + 1) * POOL_GROUP_DIM]
        f = zg
        span = 1
        while span < w:
            f = f + _shift_interleaved(f, span)
            span *= 2
        ws = _shift_interleaved(f, -(w // 2))
        start = jnp.maximum(pos - w // 2, 0)
        end = jnp.minimum(pos + w - w // 2, seq)
        cnt = jnp.maximum(end - start, 1).astype(F32)
        pooled = ws / cnt - zg
        pmg = _dot(pooled.astype(BF16), poolw_ref[g])
        pms.append(pmg * pools_ref[:, g * POOL_GROUP_DIM:(g + 1) * POOL_GROUP_DIM])
    pm_out[0] = natural(jnp.concatenate(pms, axis=1))

    z_out[0] = _dot(hm, wz_ref[...]).astype(BF16)
    g_out[0] = _dot(hm, wg_ref[...]).astype(BF16)
    gt = _dot(hm, wgt_ref[...])
    gt_t = jnp.concatenate([gt[r:r + LANES].T for r in range(0, tile, LANES)], axis=1)
    gt_out[0] = gt_t[:gt_out.shape[1]]


def _in_proj(x, n1w, wq, wk, wv, wp, wz, wg, wgt, cw, poolw, pools, heads):
    B, S, D = x.shape
    tile = IN_TILE
    dn = heads * HEAD_DIM
    pool_dim = wp.shape[1]
    xprev, xmain, xnext = _halo_specs(tile, D, S)
    out_spec = lambda width: pl.BlockSpec((1, tile, width), lambda b, j: (b, j, 0))
    return pl.pallas_call(
        functools.partial(_in_proj_kernel, tile=tile, seq=S, heads=heads),
        grid=(B, S // tile),
        in_specs=[xprev, xmain, xnext, _const_spec(n1w.shape), _const_spec(wq.shape),
                  _const_spec(wk.shape), _const_spec(wv.shape), _const_spec(wp.shape),
                  _const_spec(wz.shape), _const_spec(wg.shape), _const_spec(wgt.shape),
                  _const_spec(cw.shape), _const_spec(poolw.shape), _const_spec(pools.shape)],
        out_specs=[out_spec(dn), out_spec(dn), out_spec(dn), out_spec(pool_dim), out_spec(dn),
                   out_spec(wg.shape[1]), pl.BlockSpec((1, 4 * heads, tile), lambda b, j: (b, 0, j))],
        out_shape=[jax.ShapeDtypeStruct((B, S, dn), BF16)] * 3
        + [jax.ShapeDtypeStruct((B, S, pool_dim), BF16), jax.ShapeDtypeStruct((B, S, dn), BF16),
           jax.ShapeDtypeStruct((B, S, wg.shape[1]), BF16),
           jax.ShapeDtypeStruct((B, 4 * heads, S), F32)],
        scratch_shapes=[pltpu.VMEM((max(D, dn) // LANES, tile + 2 * HALO, LANES), F32)],
        compiler_params=pltpu.CompilerParams(dimension_semantics=("parallel", "parallel"),
                                             vmem_limit_bytes=VMEM_LIMIT),
        name="in_proj",
    )(x, x, x, n1w, wq, wk, wv, wp, wz, wg, wgt, cw, poolw, pools)


def _softplus(t):
    return jnp.maximum(t, 0.0) + jnp.log1p(jnp.exp(-jnp.abs(t)))


def _scan_sum(a, axis, reverse):
    n = a.shape[axis]
    idx = lax.broadcasted_iota(jnp.int32, a.shape, axis)
    s = 1
    while s < n:
        if reverse:
            a = a + jnp.where(idx < n - s, pltpu.roll(a, n - s, axis=axis), 0.0)
        else:
            a = a + jnp.where(idx >= s, pltpu.roll(a, s, axis=axis), 0.0)
        s *= 2
    return a


def _bmm(a, b):
    return jnp.einsum("uik,ukj->uij", a, b, preferred_element_type=F32)


def _bmm_nt(a, b):
    return jnp.einsum("uik,ujk->uij", a, b, preferred_element_type=F32)


def _delta_kernel(alog_ref, dtb_ref, q_ref, k_ref, v_ref, z_ref, gates_ref, nwc_ref, o_ref,
                  grow_s, brow_s, rhs1_a, rhs2_a, ut_a, gl_a, rhs1_b, rhs2_b, ut_b, gl_b, o_s,
                  *, nchunks, batch):
    C = CHUNK
    N = nchunks
    U = batch
    D = HEAD_DIM
    stages = N // U
    heads_here = q_ref.shape[-1] // D

    for hq in range(heads_here):
        hh = pl.program_id(1) * heads_here + hq
        for d in range(2):
            a = jnp.exp(jnp.full((1, 1), alog_ref[d, hh], F32))
            g_r = -a * _softplus(gates_ref[0, 2 + d, hq] + dtb_ref[d, hh])
            grow_s[hq, d] = _scan_sum(g_r, 1, d == 1).reshape(N, 1, C)
            brow_s[hq, d] = jax.nn.sigmoid(gates_ref[0, d, hq]).reshape(N, 1, C)

    rr = lax.broadcasted_iota(jnp.int32, (U, C, C), 1)
    cc = lax.broadcasted_iota(jnp.int32, (U, C, C), 2)

    def prep(hq, g, bufs):
        rhs1_s, rhs2_s, ut_s, gl_s = bufs
        cols = pl.ds(hq * D, D)
        for d in range(2):
            n0 = g * U if d == 0 else N - (g + 1) * U
            r0 = n0 * C
            k = k_ref[0, pl.ds(r0, U * C), cols].reshape(U, C, D)
            q = q_ref[0, pl.ds(r0, U * C), cols].reshape(U, C, D)
            kf = k.astype(F32)
            kT = jnp.swapaxes(kf, 1, 2)
            qT = jnp.swapaxes(q.astype(F32), 1, 2)
            vT = jnp.swapaxes(v_ref[0, pl.ds(r0, U * C), cols].reshape(U, C, D).astype(F32), 1, 2)
            kkq = _bmm_nt(k, jnp.concatenate([k, q], axis=1))
            KK = kkq[:, :, :C]
            QKt = kkq[:, :, C:]
            GR = grow_s[hq, d, pl.ds(n0, U)]
            BR = brow_s[hq, d, pl.ds(n0, U)]
            GC = jnp.swapaxes(jnp.broadcast_to(GR, (U, C, C)), 1, 2)
            incl = (cc >= rr) if d == 0 else (cc <= rr)
            strict = (cc > rr) if d == 0 else (cc < rr)
            decay = jnp.where(incl, jnp.exp(jnp.minimum(GR - GC, 0.0)), 0.0)
            Lt = jnp.where(strict, KK * BR * decay, 0.0)
            attn = QKt * decay
            Y = -Lt
            Lb = Lt.astype(BF16)
            P = _bmm(Lb, Lb)
            lo, hi = 0, C
            rounds = C.bit_length() - 2
            for r in range(rounds):
                m = 2 << r
                if m >= C // 4:
                    new_lo, new_hi = (0, C - m) if d == 0 else (m, C)
                    P = P[:, new_lo - lo:new_hi - lo]
                    lo, hi = new_lo, new_hi
                Pb = P.astype(BF16)
                Yb = Y.astype(BF16)
                if r < rounds - 1:
                    full = [jnp.zeros((U, lo, C), BF16), Pb, jnp.zeros((U, C - hi, C), BF16)]
                    p_full = jnp.concatenate([x for x in full if x.shape[1]], axis=1)
                    py = _bmm(Pb, jnp.concatenate([Yb, p_full], axis=2))
                    upd = Y[:, lo:hi] + P + py[:, :, :C]
                    P = py[:, :, C:]
                else:
                    upd = Y[:, lo:hi] + P + _bmm(Pb, Yb)
                parts = [Y[:, :lo], upd, Y[:, hi:]]
                Y = jnp.concatenate([x for x in parts if x.shape[1]], axis=1)
            eG = jnp.exp(GR)
            X = jnp.concatenate([kT * (BR * eG), vT * BR], axis=1)
            wu = X + _bmm(X.astype(BF16), Y.astype(BF16))
            g_last = GC[:, C - 1:C, :] if d == 0 else GC[:, 0:1, :]
            rest = jnp.exp(g_last - GR)
            kd = kf * jnp.swapaxes(jnp.broadcast_to(rest, (U, C, C)), 1, 2)
            rhs1_s[d] = jnp.concatenate([wu[:, :D], qT * eG], axis=2).astype(BF16)
            rhs2_s[d] = jnp.concatenate([attn, kd], axis=2).astype(BF16)
            ut_s[d] = wu[:, D:]
            gl_s[d] = jnp.exp(g_last)

    def finish(hq, n, ot):
        ms = jnp.mean(ot * ot, axis=0, keepdims=True)
        o = (ot * lax.rsqrt(ms + NORM_EPS) * nwc_ref[...]).T
        zz = z_ref[0, pl.ds(n * C, C), pl.ds(hq * D, D)].astype(F32)
        o_ref[0, pl.ds(n * C, C), pl.ds(hq * D, D)] = (o * (zz * jax.nn.sigmoid(zz))).astype(BF16)

    def serial(hq, g, bufs, states):
        rhs1_s, rhs2_s, ut_s, gl_s = bufs
        for j in range(U):
            new_states = []
            for d in range(2):
                jd = j if d == 0 else U - 1 - j
                step = g * U + j
                n = step if d == 0 else N - 1 - step
                ST = states[d]
                p1 = _dot(ST.astype(BF16), rhs1_s[d, jd])
                vn = ut_s[d, jd] - p1[:, :C]
                p2 = _dot(vn.astype(BF16), rhs2_s[d, jd])
                ot = p1[:, C:] + p2[:, :C]
                if step < N // 2:
                    o_s[n] = ot
                else:
                    finish(hq, n, ot + o_s[n])
                new_states.append(ST * gl_s[d, jd] + p2[:, C:])
            states = tuple(new_states)
        return states

    bufs = ((rhs1_a, rhs2_a, ut_a, gl_a), (rhs1_b, rhs2_b, ut_b, gl_b))
    zero = jnp.zeros((D, D), F32)
    work = [(hq, g) for hq in range(heads_here) for g in range(stages)]
    prep(*work[0], bufs[0])
    for i, (hq, g) in enumerate(work):
        if i + 1 < len(work):
            prep(*work[i + 1], bufs[(i + 1) % 2])
        states = serial(hq, g, bufs[i % 2], (zero, zero) if g == 0 else states)


def _delta(q, k, v, z, gates_rows, a_log, dt_bias, nw_cols):
    B, S, dn = q.shape
    heads = dn // HEAD_DIM
    N = S // CHUNK
    U = DELTA_BATCH
    assert N % U == 0 and N % 2 == 0 and U % 2 == 0
    hp = DELTA_HEADS
    assert heads % hp == 0
    head_spec = pl.BlockSpec((1, S, hp * HEAD_DIM), lambda b, h: (b, 0, h))
    smem = pl.BlockSpec(memory_space=pltpu.SMEM)
    return pl.pallas_call(
        functools.partial(_delta_kernel, nchunks=N, batch=DELTA_BATCH),
        grid=(B, heads // hp),
        in_specs=[smem, smem, head_spec, head_spec, head_spec, head_spec,
                  pl.BlockSpec((1, 4, hp, N, CHUNK), lambda b, h: (b, 0, h, 0, 0)),
                  pl.BlockSpec((HEAD_DIM, LANES), lambda b, h: (0, 0))],
        out_specs=head_spec,
        out_shape=jax.ShapeDtypeStruct((B, S, dn), BF16),
        scratch_shapes=[
            pltpu.VMEM((hp, 2, N, 1, CHUNK), F32),
            pltpu.VMEM((hp, 2, N, 1, CHUNK), F32),
        ] + 2 * [
            pltpu.VMEM((2, U, HEAD_DIM, 2 * CHUNK), BF16),
            pltpu.VMEM((2, U, CHUNK, CHUNK + HEAD_DIM), BF16),
            pltpu.VMEM((2, U, HEAD_DIM, CHUNK), F32),
            pltpu.VMEM((2, U, 1, LANES), F32),
        ] + [pltpu.VMEM((N, HEAD_DIM, CHUNK), F32)],
        compiler_params=pltpu.CompilerParams(dimension_semantics=("parallel", "parallel"),
                                             vmem_limit_bytes=VMEM_LIMIT),
        name="delta_rule",
    )(a_log, dt_bias, q, k, v, z, gates_rows, nw_cols)


def _merge_kernel(o_ref, pm_ref, g_ref, x_ref, dn_ref, po_ref, wo_ref, out_ref):
    d = out_ref.shape[-1]
    y_dn = _dot(o_ref[0], dn_ref[...])
    y_pool = _dot(pm_ref[0], po_ref[...])
    g = g_ref[0].astype(F32)
    merged = jax.nn.sigmoid(g[:, :d]) * y_pool + jax.nn.sigmoid(g[:, d:]) * y_dn
    out_ref[0] = x_ref[0] + _dot(merged.astype(BF16), wo_ref[...])


def _merge(o, pm, g, x, dn_out, pool_out, w_o):
    B, S, D = x.shape
    tile = MERGE_TILE
    spec = lambda width: pl.BlockSpec((1, tile, width), lambda b, j: (b, j, 0))
    return pl.pallas_call(
        _merge_kernel,
        grid=(B, S // tile),
        in_specs=[spec(o.shape[-1]), spec(pm.shape[-1]), spec(g.shape[-1]), spec(D),
                  _const_spec(dn_out.shape), _const_spec(pool_out.shape), _const_spec(w_o.shape)],
        out_specs=spec(D),
        out_shape=jax.ShapeDtypeStruct((B, S, D), F32),
        compiler_params=pltpu.CompilerParams(dimension_semantics=("parallel", "parallel"),
                                             vmem_limit_bytes=VMEM_LIMIT),
        name="merge",
    )(o, pm, g, x, dn_out, pool_out, w_o)


def _ffn_kernel(xp_ref, xm_ref, xn_ref, n2w_ref, up_ref, cw_ref, cb_ref, down_ref, fw_ref, out_ref,
                act_s, il_s, *, tile):
    j = pl.program_id(1)
    nj = pl.num_programs(1)
    n = tile + 2 * HALO
    hidden = down_ref.shape[0]
    xm = xm_ref[0]
    xe = jnp.concatenate([xp_ref[0], xm, xn_ref[0]], axis=0)
    h = jnp.where(_halo_valid(n, tile, j, nj), _rms(xe, n2w_ref[...]), 0.0)
    hb = _interleave(il_s, h).astype(BF16)

    def conv_up(col):
        cols = slice(col, col + FFN_CHUNK)
        u = _dot(hb, up_ref[:, cols])
        acc = None
        for tt in range(FFN_CONV):
            term = cw_ref[tt:tt + 1, cols] * _shift_interleaved(u, tt - FFN_CONV // 2)
            acc = term if acc is None else acc + term
        return acc + cb_ref[:, cols]

    for col in range(0, hidden, FFN_CHUNK):
        gate = conv_up(col)
        val = conv_up(hidden + col)
        act_s[:, col:col + FFN_CHUNK] = (gate * jax.nn.sigmoid(gate) * val).astype(BF16)

    y = _deinterleave(il_s, _dot(act_s[...], down_ref[...]), HALO, HALO + tile)
    out_ref[0] = _rms(xm + y, fw_ref[...])


def _ffn(x, n2w, up, cw, cb, down, fw):
    B, S, D = x.shape
    tile = FFN_TILE
    hidden = down.shape[0]
    xprev, xmain, xnext = _halo_specs(tile, D, S)
    return pl.pallas_call(
        functools.partial(_ffn_kernel, tile=tile),
        grid=(B, S // tile),
        in_specs=[xprev, xmain, xnext, _const_spec(n2w.shape), _const_spec(up.shape),
                  _const_spec(cw.shape), _const_spec(cb.shape), _const_spec(down.shape),
                  _const_spec(fw.shape)],
        out_specs=pl.BlockSpec((1, tile, D), lambda b, j: (b, j, 0)),
        out_shape=jax.ShapeDtypeStruct((B, S, D), F32),
        scratch_shapes=[pltpu.VMEM((tile + 2 * HALO, hidden), BF16),
                        pltpu.VMEM((D // LANES, tile + 2 * HALO, LANES), F32)],
        compiler_params=pltpu.CompilerParams(dimension_semantics=("parallel", "parallel"),
                                             vmem_limit_bytes=VMEM_LIMIT),
        name="ffn",
    )(x, x, x, n2w, up, cw, cb, down, fw)


def kernel(x, norm1_w, w_in, pool_w, pool_scale, pool_out, qkv_conv_w, a_log, dt_bias, dn_norm_w,
           dn_out, w_o, norm2_w, ffn_up, ffn_conv_w, ffn_conv_b, ffn_down, final_norm_w):
    B, S, D = x.shape
    depth = w_in.shape[0]
    pool_dim = pool_out.shape[1]
    dn = dn_out.shape[1]
    heads = a_log.shape[-1]
    F = ffn_down.shape[1]
    assert depth == 1, "the final rmsnorm is fused into the (single) layer's channel mixer"
    assert dn == heads * HEAD_DIM and CHUNK == HEAD_DIM == LANES
    assert S % (CHUNK * DELTA_BATCH) == 0 and S // CHUNK <= LANES
    assert S % IN_TILE == 0 and S % MERGE_TILE == 0 and S % FFN_TILE == 0 and F % FFN_CHUNK == 0
    N = S // CHUNK
    row = lambda t: t.reshape(1, -1).astype(F32)

    for l in range(depth):
        wl = w_in[l].astype(BF16)
        c0 = pool_dim
        wp = wl[:, :c0]
        wq, wk, wv, wz = (wl[:, c0 + i * dn:c0 + (i + 1) * dn] for i in range(4))
        c1 = c0 + 4 * dn
        wgt = jnp.pad(wl[:, c1:c1 + 4 * heads], ((0, 0), (0, LANES - 4 * heads)))
        wg = wl[:, c1 + 4 * heads:]

        q, k, v, pm, z, g, gates = _in_proj(
            x, row(norm1_w[l]), wq, wk, wv, wp, wz, wg, wgt, qkv_conv_w[l].astype(F32),
            pool_w[l].astype(BF16), row(pool_scale[l]), heads)

        gates_rows = gates.reshape(B, 4, heads, N, CHUNK)
        nw_cols = jnp.broadcast_to(dn_norm_w[l].astype(F32)[:, None], (HEAD_DIM, LANES))
        o = _delta(q, k, v, z, gates_rows, a_log[l].astype(F32), dt_bias[l].astype(F32), nw_cols)

        x = _merge(o, pm, g, x, dn_out[l].astype(BF16), pool_out[l].astype(BF16), w_o[l].astype(BF16))

        x = _ffn(x, row(norm2_w[l]), ffn_up[l].astype(BF16), ffn_conv_w[l].astype(F32),
                 row(ffn_conv_b[l]), ffn_down[l].astype(BF16), row(final_norm_w))
    return x
```

```python
import functools

import jax
import jax.numpy as jnp
from jax import lax
from jax.experimental import pallas as pl
from jax.experimental.pallas import tpu as pltpu

F32 = jnp.float32
BF16 = jnp.bfloat16

NORM_EPS = 1e-6
L2_EPS = 1e-6
POOL_WINDOWS = (2, 4, 8, 16)
POOL_GROUP_DIM = 128
HEAD_DIM = 128
SHORT_CONV = 5
FFN_CONV = 3

SUBLANES = 8
LANES = 128
HALO = SUBLANES
CHUNK = 128
DELTA_BATCH = 8
DELTA_HEADS = 2
VMEM_LIMIT = 56 * 1024 * 1024

IN_TILE = 512
MERGE_TILE = 1024
FFN_TILE = 512
FFN_CHUNK = 256


def _interleave(scr, val):
    n, width = val.shape
    pitch = n // SUBLANES
    blocks = width // LANES
    for c in range(blocks):
        scr[c] = val[:, c * LANES:(c + 1) * LANES]
    slabs = [jnp.concatenate([scr[c, pl.ds(a, SUBLANES, stride=pitch), :] for c in range(blocks)], axis=1)
             for a in range(pitch)]
    return jnp.concatenate(slabs, axis=0)


def _deinterleave(scr, val, lo, hi):
    n, width = val.shape
    pitch = n // SUBLANES
    blocks = width // LANES
    for a in range(pitch):
        for c in range(blocks):
            scr[c, pl.ds(a, SUBLANES, stride=pitch), :] = val[a * SUBLANES:(a + 1) * SUBLANES,
                                                             c * LANES:(c + 1) * LANES]
    return jnp.concatenate([scr[c, lo:hi, :] for c in range(blocks)], axis=1)


def _shift_interleaved(t, s):
    n = t.shape[0]
    if s == 0:
        return t
    m = abs(s)
    if s > 0:
        edge = [pltpu.roll(t[SUBLANES * j:SUBLANES * (j + 1)], SUBLANES - 1, axis=0) for j in range(m)]
        return jnp.concatenate([t[SUBLANES * m:]] + edge, axis=0)
    edge = [pltpu.roll(t[n - SUBLANES * j:n - SUBLANES * (j - 1)], 1, axis=0) for j in range(m, 0, -1)]
    return jnp.concatenate(edge + [t[:n - SUBLANES * m]], axis=0)


def _dot(a, b):
    return jnp.dot(a, b, preferred_element_type=F32)


def _rms(xv, w):
    ms = jnp.mean(xv * xv, axis=-1, keepdims=True)
    return xv * lax.rsqrt(ms + NORM_EPS) * w


def _halo_valid(n_rows, tile, j, nj):
    row = lax.broadcasted_iota(jnp.int32, (n_rows, 1), 0)
    top_ok = jnp.logical_or(row >= HALO, j > 0)
    bot_ok = jnp.logical_or(row < tile + HALO, j < nj - 1)
    return jnp.logical_and(top_ok, bot_ok)


def _halo_specs(tile, width, seq):
    per = tile // HALO
    last = seq // HALO - 1
    prev = pl.BlockSpec((1, HALO, width), lambda b, j: (b, jnp.maximum(j * per - 1, 0), 0))
    main = pl.BlockSpec((1, tile, width), lambda b, j: (b, j, 0))
    nxt = pl.BlockSpec((1, HALO, width), lambda b, j: (b, jnp.minimum((j + 1) * per, last), 0))
    return prev, main, nxt


def _const_spec(shape):
    nd = len(shape)
    return pl.BlockSpec(shape, lambda *_: (0,) * nd, pipeline_mode=pl.Buffered(1))


def _in_proj_kernel(xp_ref, xm_ref, xn_ref, n1w_ref, wq_ref, wk_ref, wv_ref, wp_ref, wz_ref,
                    wg_ref, wgt_ref, cw_ref, poolw_ref, pools_ref,
                    q_out, k_out, v_out, pm_out, z_out, g_out, gt_out, il_s, *, tile, seq, heads):
    j = pl.program_id(1)
    nj = pl.num_programs(1)
    n = tile + 2 * HALO
    dn = heads * HEAD_DIM
    xe = jnp.concatenate([xp_ref[0], xm_ref[0], xn_ref[0]], axis=0)
    h = _rms(xe, n1w_ref[...])
    h = jnp.where(_halo_valid(n, tile, j, nj), h, 0.0)
    hm = h[HALO:HALO + tile].astype(BF16)
    hi = _interleave(il_s, h).astype(BF16)

    def conv_silu(w_ref, col0):
        t = _dot(hi, w_ref[...])
        acc = None
        for tt in range(SHORT_CONV):
            term = cw_ref[tt:tt + 1, col0:col0 + dn] * _shift_interleaved(t, tt - SHORT_CONV // 2)
            acc = term if acc is None else acc + term
        return acc * jax.nn.sigmoid(acc)

    def l2n(y, scale):
        outs = []
        for hh in range(heads):
            th = y[:, hh * HEAD_DIM:(hh + 1) * HEAD_DIM]
            ss = jnp.sum(th * th, axis=-1, keepdims=True)
            outs.append(th * (lax.rsqrt(ss + L2_EPS) * scale))
        return jnp.concatenate(outs, axis=-1)

    def natural(y):
        return _deinterleave(il_s, y, HALO, HALO + tile).astype(BF16)

    q_out[0] = natural(l2n(conv_silu(wq_ref, 0), HEAD_DIM ** -0.5))
    k_out[0] = natural(l2n(conv_silu(wk_ref, dn), 1.0))
    v_out[0] = natural(conv_silu(wv_ref, 2 * dn))

    pe = _dot(hi, wp_ref[...])
    slot = lax.broadcasted_iota(jnp.int32, (n, 1), 0)
    pos = j * tile - HALO + (slot // SUBLANES) + (n // SUBLANES) * (slot % SUBLANES)
    pms = []
    for g, w in enumerate(POOL_WINDOWS):
        zg = pe[:, g * POOL_GROUP_DIM:(g + 1) * POOL_GROUP_DIM]
        f = zg
        span = 1
        while span < w:
            f = f + _shift_interleaved(f, span)
            span *= 2
        ws = _shift_interleaved(f, -(w // 2))
        start = jnp.maximum(pos - w // 2, 0)
        end = jnp.minimum(pos + w - w // 2, seq)
        cnt = jnp.maximum(end - start, 1).astype(F32)
        pooled = ws / cnt - zg
        pmg = _dot(pooled.astype(BF16), poolw_ref[g])
        pms.append(pmg * pools_ref[:, g * POOL_GROUP_DIM:(g + 1) * POOL_GROUP_DIM])
    pm_out[0] = natural(jnp.concatenate(pms, axis=1))

    z_out[0] = _dot(hm, wz_ref[...]).astype(BF16)
    g_out[0] = _dot(hm, wg_ref[...]).astype(BF16)
    gt = _dot(hm, wgt_ref[...])
    gt_t = jnp.concatenate([gt[r:r + LANES].T for r in range(0, tile, LANES)], axis=1)
    gt_out[0] = gt_t[:gt_out.shape[1]]


def _in_proj(x, n1w, wq, wk, wv, wp, wz, wg, wgt, cw, poolw, pools, heads):
    B, S, D = x.shape
    tile = IN_TILE
    dn = heads * HEAD_DIM
    pool_dim = wp.shape[1]
    xprev, xmain, xnext = _halo_specs(tile, D, S)
    out_spec = lambda width: pl.BlockSpec((1, tile, width), lambda b, j: (b, j, 0))
    return pl.pallas_call(
        functools.partial(_in_proj_kernel, tile=tile, seq=S, heads=heads),
        grid=(B, S // tile),
        in_specs=[xprev, xmain, xnext, _const_spec(n1w.shape), _const_spec(wq.shape),
                  _const_spec(wk.shape), _const_spec(wv.shape), _const_spec(wp.shape),
                  _const_spec(wz.shape), _const_spec(wg.shape), _const_spec(wgt.shape),
                  _const_spec(cw.shape), _const_spec(poolw.shape), _const_spec(pools.shape)],
        out_specs=[out_spec(dn), out_spec(dn), out_spec(dn), out_spec(pool_dim), out_spec(dn),
                   out_spec(wg.shape[1]), pl.BlockSpec((1, 4 * heads, tile), lambda b, j: (b, 0, j))],
        out_shape=[jax.ShapeDtypeStruct((B, S, dn), BF16)] * 3
        + [jax.ShapeDtypeStruct((B, S, pool_dim), BF16), jax.ShapeDtypeStruct((B, S, dn), BF16),
           jax.ShapeDtypeStruct((B, S, wg.shape[1]), BF16),
           jax.ShapeDtypeStruct((B, 4 * heads, S), F32)],
        scratch_shapes=[pltpu.VMEM((max(D, dn) // LANES, tile + 2 * HALO, LANES), F32)],
        compiler_params=pltpu.CompilerParams(dimension_semantics=("parallel", "parallel"),
                                             vmem_limit_bytes=VMEM_LIMIT),
        name="in_proj",
    )(x, x, x, n1w, wq, wk, wv, wp, wz, wg, wgt, cw, poolw, pools)


def _softplus(t):
    return jnp.maximum(t, 0.0) + jnp.log1p(jnp.exp(-jnp.abs(t)))


def _scan_sum(a, axis, reverse):
    n = a.shape[axis]
    idx = lax.broadcasted_iota(jnp.int32, a.shape, axis)
    s = 1
    while s < n:
        if reverse:
            a = a + jnp.where(idx < n - s, pltpu.roll(a, n - s, axis=axis), 0.0)
        else:
            a = a + jnp.where(idx >= s, pltpu.roll(a, s, axis=axis), 0.0)
        s *= 2
    return a


def _bmm(a, b):
    return jnp.einsum("uik,ukj->uij", a, b, preferred_element_type=F32)


def _bmm_nt(a, b):
    return jnp.einsum("uik,ujk->uij", a, b, preferred_element_type=F32)


def _delta_kernel(alog_ref, dtb_ref, q_ref, k_ref, v_ref, z_ref, gates_ref, nwc_ref, o_ref,
                  gtmp_s, grow_s, brow_s, rhs1_a, rhs2_a, ut_a, gl_a, rhs1_b, rhs2_b, ut_b, gl_b, o_s,
                  *, nchunks, batch):
    C = CHUNK
    N = nchunks
    U = batch
    D = HEAD_DIM
    stages = N // U
    heads_here = q_ref.shape[-1] // D

    for hq in range(heads_here):
        hh = pl.program_id(1) * heads_here + hq
        mine = lax.broadcasted_iota(jnp.int32, (gates_ref.shape[2], C), 0) == hh
        for t in range(4):
            for n in range(N):
                every_head = gates_ref[0, t, :, pl.ds(n * C, C)]
                gtmp_s[t, pl.ds(n, 1), :] = jnp.sum(jnp.where(mine, every_head, 0.0), axis=0, keepdims=True)
        for d in range(2):
            a = jnp.exp(jnp.full((1, 1), alog_ref[d, hh], F32))
            g_r = -a * _softplus(gtmp_s[2 + d] + dtb_ref[d, hh])
            grow_s[hq, d] = _scan_sum(g_r, 1, d == 1).reshape(N, 1, C)
            brow_s[hq, d] = jax.nn.sigmoid(gtmp_s[d]).reshape(N, 1, C)

    rr = lax.broadcasted_iota(jnp.int32, (U, C, C), 1)
    cc = lax.broadcasted_iota(jnp.int32, (U, C, C), 2)

    def prep(hq, g, bufs):
        rhs1_s, rhs2_s, ut_s, gl_s = bufs
        cols = pl.ds(hq * D, D)
        for d in range(2):
            n0 = g * U if d == 0 else N - (g + 1) * U
            r0 = n0 * C
            k = k_ref[0, pl.ds(r0, U * C), cols].reshape(U, C, D)
            q = q_ref[0, pl.ds(r0, U * C), cols].reshape(U, C, D)
            kf = k.astype(F32)
            kT = jnp.swapaxes(kf, 1, 2)
            qT = jnp.swapaxes(q.astype(F32), 1, 2)
            vT = jnp.swapaxes(v_ref[0, pl.ds(r0, U * C), cols].reshape(U, C, D).astype(F32), 1, 2)
            kkq = _bmm_nt(k, jnp.concatenate([k, q], axis=1))
            KK = kkq[:, :, :C]
            QKt = kkq[:, :, C:]
            GR = grow_s[hq, d, pl.ds(n0, U)]
            BR = brow_s[hq, d, pl.ds(n0, U)]
            GC = jnp.swapaxes(jnp.broadcast_to(GR, (U, C, C)), 1, 2)
            incl = (cc >= rr) if d == 0 else (cc <= rr)
            strict = (cc > rr) if d == 0 else (cc < rr)
            decay = jnp.where(incl, jnp.exp(jnp.minimum(GR - GC, 0.0)), 0.0)
            Lt = jnp.where(strict, KK * BR * decay, 0.0)
            attn = QKt * decay
            Y = -Lt
            Lb = Lt.astype(BF16)
            P = _bmm(Lb, Lb)
            lo, hi = 0, C
            rounds = C.bit_length() - 2
            for r in range(rounds):
                m = 2 << r
                if m >= C // 4:
                    new_lo, new_hi = (0, C - m) if d == 0 else (m, C)
                    P = P[:, new_lo - lo:new_hi - lo]
                    lo, hi = new_lo, new_hi
                Pb = P.astype(BF16)
                Yb = Y.astype(BF16)
                if r < rounds - 1:
                    full = [jnp.zeros((U, lo, C), BF16), Pb, jnp.zeros((U, C - hi, C), BF16)]
                    p_full = jnp.concatenate([x for x in full if x.shape[1]], axis=1)
                    py = _bmm(Pb, jnp.concatenate([Yb, p_full], axis=2))
                    upd = Y[:, lo:hi] + P + py[:, :, :C]
                    P = py[:, :, C:]
                else:
                    upd = Y[:, lo:hi] + P + _bmm(Pb, Yb)
                parts = [Y[:, :lo], upd, Y[:, hi:]]
                Y = jnp.concatenate([x for x in parts if x.shape[1]], axis=1)
            eG = jnp.exp(GR)
            X = jnp.concatenate([kT * (BR * eG), vT * BR], axis=1)
            wu = X + _bmm(X.astype(BF16), Y.astype(BF16))
            g_last = GC[:, C - 1:C, :] if d == 0 else GC[:, 0:1, :]
            rest = jnp.exp(g_last - GR)
            kd = kf * jnp.swapaxes(jnp.broadcast_to(rest, (U, C, C)), 1, 2)
            rhs1_s[d] = jnp.concatenate([wu[:, :D], qT * eG], axis=2).astype(BF16)
            rhs2_s[d] = jnp.concatenate([attn, kd], axis=2).astype(BF16)
            ut_s[d] = wu[:, D:]
            gl_s[d] = jnp.exp(g_last)

    def finish(hq, n, ot):
        ms = jnp.mean(ot * ot, axis=0, keepdims=True)
        o = (ot * lax.rsqrt(ms + NORM_EPS) * nwc_ref[...]).T
        zz = z_ref[0, pl.ds(n * C, C), pl.ds(hq * D, D)].astype(F32)
        o_ref[0, pl.ds(n * C, C), pl.ds(hq * D, D)] = (o * (zz * jax.nn.sigmoid(zz))).astype(BF16)

    def serial(hq, g, bufs, states):
        rhs1_s, rhs2_s, ut_s, gl_s = bufs
        for j in range(U):
            new_states = []
            for d in range(2):
                jd = j if d == 0 else U - 1 - j
                step = g * U + j
                n = step if d == 0 else N - 1 - step
                ST = states[d]
                p1 = _dot(ST.astype(BF16), rhs1_s[d, jd])
                vn = ut_s[d, jd] - p1[:, :C]
                p2 = _dot(vn.astype(BF16), rhs2_s[d, jd])
                ot = p1[:, C:] + p2[:, :C]
                if step < N // 2:
                    o_s[n] = ot
                else:
                    finish(hq, n, ot + o_s[n])
                new_states.append(ST * gl_s[d, jd] + p2[:, C:])
            states = tuple(new_states)
        return states

    bufs = ((rhs1_a, rhs2_a, ut_a, gl_a), (rhs1_b, rhs2_b, ut_b, gl_b))
    zero = jnp.zeros((D, D), F32)
    work = [(hq, g) for hq in range(heads_here) for g in range(stages)]
    prep(*work[0], bufs[0])
    for i, (hq, g) in enumerate(work):
        if i + 1 < len(work):
            prep(*work[i + 1], bufs[(i + 1) % 2])
        states = serial(hq, g, bufs[i % 2], (zero, zero) if g == 0 else states)


def _delta(q, k, v, z, gates, a_log, dt_bias, nw_cols):
    B, S, dn = q.shape
    heads = dn // HEAD_DIM
    N = S // CHUNK
    U = DELTA_BATCH
    assert N % U == 0 and N % 2 == 0 and U % 2 == 0
    hp = DELTA_HEADS
    assert heads % hp == 0
    head_spec = pl.BlockSpec((1, S, hp * HEAD_DIM), lambda b, h: (b, 0, h))
    smem = pl.BlockSpec(memory_space=pltpu.SMEM)
    return pl.pallas_call(
        functools.partial(_delta_kernel, nchunks=N, batch=DELTA_BATCH),
        grid=(B, heads // hp),
        in_specs=[smem, smem, head_spec, head_spec, head_spec, head_spec,
                  pl.BlockSpec((1, 4, heads, S), lambda b, h: (b, 0, 0, 0)),
                  pl.BlockSpec((HEAD_DIM, LANES), lambda b, h: (0, 0))],
        out_specs=head_spec,
        out_shape=jax.ShapeDtypeStruct((B, S, dn), BF16),
        scratch_shapes=[
            pltpu.VMEM((4, N, CHUNK), F32),
            pltpu.VMEM((hp, 2, N, 1, CHUNK), F32),
            pltpu.VMEM((hp, 2, N, 1, CHUNK), F32),
        ] + 2 * [
            pltpu.VMEM((2, U, HEAD_DIM, 2 * CHUNK), BF16),
            pltpu.VMEM((2, U, CHUNK, CHUNK + HEAD_DIM), BF16),
            pltpu.VMEM((2, U, HEAD_DIM, CHUNK), F32),
            pltpu.VMEM((2, U, 1, LANES), F32),
        ] + [pltpu.VMEM((N, HEAD_DIM, CHUNK), F32)],
        compiler_params=pltpu.CompilerParams(dimension_semantics=("parallel", "parallel"),
                                             vmem_limit_bytes=VMEM_LIMIT),
        name="delta_rule",
    )(a_log, dt_bias, q, k, v, z, gates, nw_cols)


def _merge_kernel(o_ref, pm_ref, g_ref, x_ref, dn_ref, po_ref, wo_ref, out_ref):
    d = out_ref.shape[-1]
    y_dn = _dot(o_ref[0], dn_ref[...])
    y_pool = _dot(pm_ref[0], po_ref[...])
    g = g_ref[0].astype(F32)
    merged = jax.nn.sigmoid(g[:, :d]) * y_pool + jax.nn.sigmoid(g[:, d:]) * y_dn
    out_ref[0] = x_ref[0] + _dot(merged.astype(BF16), wo_ref[...])


def _merge(o, pm, g, x, dn_out, pool_out, w_o):
    B, S, D = x.shape
    tile = MERGE_TILE
    spec = lambda width: pl.BlockSpec((1, tile, width), lambda b, j: (b, j, 0))
    return pl.pallas_call(
        _merge_kernel,
        grid=(B, S // tile),
        in_specs=[spec(o.shape[-1]), spec(pm.shape[-1]), spec(g.shape[-1]), spec(D),
                  _const_spec(dn_out.shape), _const_spec(pool_out.shape), _const_spec(w_o.shape)],
        out_specs=spec(D),
        out_shape=jax.ShapeDtypeStruct((B, S, D), F32),
        compiler_params=pltpu.CompilerParams(dimension_semantics=("parallel", "parallel"),
                                             vmem_limit_bytes=VMEM_LIMIT),
        name="merge",
    )(o, pm, g, x, dn_out, pool_out, w_o)


def _ffn_kernel(xp_ref, xm_ref, xn_ref, n2w_ref, up_ref, cw_ref, cb_ref, down_ref, fw_ref, out_ref,
                act_s, il_s, *, tile):
    j = pl.program_id(1)
    nj = pl.num_programs(1)
    n = tile + 2 * HALO
    hidden = down_ref.shape[0]
    xm = xm_ref[0]
    xe = jnp.concatenate([xp_ref[0], xm, xn_ref[0]], axis=0)
    h = jnp.where(_halo_valid(n, tile, j, nj), _rms(xe, n2w_ref[...]), 0.0)
    hb = _interleave(il_s, h).astype(BF16)

    def conv_up(col):
        cols = slice(col, col + FFN_CHUNK)
        u = _dot(hb, up_ref[:, cols])
        acc = None
        for tt in range(FFN_CONV):
            term = cw_ref[tt:tt + 1, cols] * _shift_interleaved(u, tt - FFN_CONV // 2)
            acc = term if acc is None else acc + term
        return acc + cb_ref[:, cols]

    for col in range(0, hidden, FFN_CHUNK):
        gate = conv_up(col)
        val = conv_up(hidden + col)
        act_s[:, col:col + FFN_CHUNK] = (gate * jax.nn.sigmoid(gate) * val).astype(BF16)

    y = _deinterleave(il_s, _dot(act_s[...], down_ref[...]), HALO, HALO + tile)
    out_ref[0] = _rms(xm + y, fw_ref[...])


def _ffn(x, n2w, up, cw, cb, down, fw):
    B, S, D = x.shape
    tile = FFN_TILE
    hidden = down.shape[0]
    xprev, xmain, xnext = _halo_specs(tile, D, S)
    return pl.pallas_call(
        functools.partial(_ffn_kernel, tile=tile),
        grid=(B, S // tile),
        in_specs=[xprev, xmain, xnext, _const_spec(n2w.shape), _const_spec(up.shape),
                  _const_spec(cw.shape), _const_spec(cb.shape), _const_spec(down.shape),
                  _const_spec(fw.shape)],
        out_specs=pl.BlockSpec((1, tile, D), lambda b, j: (b, j, 0)),
        out_shape=jax.ShapeDtypeStruct((B, S, D), F32),
        scratch_shapes=[pltpu.VMEM((tile + 2 * HALO, hidden), BF16),
                        pltpu.VMEM((D // LANES, tile + 2 * HALO, LANES), F32)],
        compiler_params=pltpu.CompilerParams(dimension_semantics=("parallel", "parallel"),
                                             vmem_limit_bytes=VMEM_LIMIT),
        name="ffn",
    )(x, x, x, n2w, up, cw, cb, down, fw)


def kernel(x, norm1_w, w_in, pool_w, pool_scale, pool_out, qkv_conv_w, a_log, dt_bias, dn_norm_w,
           dn_out, w_o, norm2_w, ffn_up, ffn_conv_w, ffn_conv_b, ffn_down, final_norm_w):
    B, S, D = x.shape
    depth = w_in.shape[0]
    pool_dim = pool_out.shape[1]
    dn = dn_out.shape[1]
    heads = a_log.shape[-1]
    F = ffn_down.shape[1]
    assert depth == 1, "the final rmsnorm is fused into the (single) layer's channel mixer"
    assert dn == heads * HEAD_DIM and CHUNK == HEAD_DIM == LANES
    assert S % (CHUNK * DELTA_BATCH) == 0 and S // CHUNK <= LANES
    assert S % IN_TILE == 0 and S % MERGE_TILE == 0 and S % FFN_TILE == 0 and F % FFN_CHUNK == 0
    N = S // CHUNK
    row = lambda t: t.reshape(1, -1).astype(F32)

    for l in range(depth):
        wl = w_in[l].astype(BF16)
        c0 = pool_dim
        wp = wl[:, :c0]
        wq, wk, wv, wz = (wl[:, c0 + i * dn:c0 + (i + 1) * dn] for i in range(4))
        c1 = c0 + 4 * dn
        wgt = jnp.pad(wl[:, c1:c1 + 4 * heads], ((0, 0), (0, LANES - 4 * heads)))
        wg = wl[:, c1 + 4 * heads:]

        q, k, v, pm, z, g, gates = _in_proj(
            x, row(norm1_w[l]), wq, wk, wv, wp, wz, wg, wgt, qkv_conv_w[l].astype(F32),
            pool_w[l].astype(BF16), row(pool_scale[l]), heads)

        nw_cols = jnp.broadcast_to(dn_norm_w[l].astype(F32)[:, None], (HEAD_DIM, LANES))
        o = _delta(q, k, v, z, gates.reshape(B, 4, heads, S), a_log[l].astype(F32), dt_bias[l].astype(F32),
                   nw_cols)

        x = _merge(o, pm, g, x, dn_out[l].astype(BF16), pool_out[l].astype(BF16), w_o[l].astype(BF16))

        x = _ffn(x, row(norm2_w[l]), ffn_up[l].astype(BF16), ffn_conv_w[l].astype(F32),
                 row(ffn_conv_b[l]), ffn_down[l].astype(BF16), row(final_norm_w))
    return x
```

```python
import functools

import jax
import jax.numpy as jnp
from jax import lax
from jax.experimental import pallas as pl
from jax.experimental.pallas import tpu as pltpu

F32 = jnp.float32
BF16 = jnp.bfloat16

NORM_EPS = 1e-6
L2_EPS = 1e-6
POOL_WINDOWS = (2, 4, 8, 16)
POOL_GROUP_DIM = 128
HEAD_DIM = 128
SHORT_CONV = 5
FFN_CONV = 3

SUBLANES = 8
LANES = 128
HALO = SUBLANES
CHUNK = 128
DELTA_BATCH = 8
DELTA_HEADS = 2
VMEM_LIMIT = 56 * 1024 * 1024

IN_TILE = 512
MERGE_TILE = 1024
FFN_TILE = 512
FFN_CHUNK = 256


def _interleave(scr, val):
    n, width = val.shape
    pitch = n // SUBLANES
    blocks = width // LANES
    for c in range(blocks):
        scr[c] = val[:, c * LANES:(c + 1) * LANES]
    slabs = [jnp.concatenate([scr[c, pl.ds(a, SUBLANES, stride=pitch), :] for c in range(blocks)], axis=1)
             for a in range(pitch)]
    return jnp.concatenate(slabs, axis=0)


def _deinterleave(scr, val, lo, hi):
    n, width = val.shape
    pitch = n // SUBLANES
    blocks = width // LANES
    for a in range(pitch):
        for c in range(blocks):
            scr[c, pl.ds(a, SUBLANES, stride=pitch), :] = val[a * SUBLANES:(a + 1) * SUBLANES,
                                                             c * LANES:(c + 1) * LANES]
    return jnp.concatenate([scr[c, lo:hi, :] for c in range(blocks)], axis=1)


def _shift_interleaved(t, s):
    n = t.shape[0]
    if s == 0:
        return t
    m = abs(s)
    if s > 0:
        edge = [pltpu.roll(t[SUBLANES * j:SUBLANES * (j + 1)], SUBLANES - 1, axis=0) for j in range(m)]
        return jnp.concatenate([t[SUBLANES * m:]] + edge, axis=0)
    edge = [pltpu.roll(t[n - SUBLANES * j:n - SUBLANES * (j - 1)], 1, axis=0) for j in range(m, 0, -1)]
    return jnp.concatenate(edge + [t[:n - SUBLANES * m]], axis=0)


def _dot(a, b):
    return jnp.dot(a, b, preferred_element_type=F32)


def _rms(xv, w):
    ms = jnp.mean(xv * xv, axis=-1, keepdims=True)
    return xv * lax.rsqrt(ms + NORM_EPS) * w


def _halo_valid(n_rows, tile, j, nj):
    row = lax.broadcasted_iota(jnp.int32, (n_rows, 1), 0)
    top_ok = jnp.logical_or(row >= HALO, j > 0)
    bot_ok = jnp.logical_or(row < tile + HALO, j < nj - 1)
    return jnp.logical_and(top_ok, bot_ok)


def _halo_specs(tile, width, seq):
    per = tile // HALO
    last = seq // HALO - 1
    prev = pl.BlockSpec((1, HALO, width), lambda b, j: (b, jnp.maximum(j * per - 1, 0), 0))
    main = pl.BlockSpec((1, tile, width), lambda b, j: (b, j, 0))
    nxt = pl.BlockSpec((1, HALO, width), lambda b, j: (b, jnp.minimum((j + 1) * per, last), 0))
    return prev, main, nxt


SINGLE_BUFFER_ELEMS = 1 << 18


def _const_spec(shape):
    nd = len(shape)
    size = 1
    for s in shape:
        size *= s
    mode = {"pipeline_mode": pl.Buffered(1)} if size >= SINGLE_BUFFER_ELEMS else {}
    return pl.BlockSpec(shape, lambda *_: (0,) * nd, **mode)


def _in_proj_kernel(xp_ref, xm_ref, xn_ref, n1w_ref, wq_ref, wk_ref, wv_ref, wp_ref, wz_ref,
                    wg_ref, wgt_ref, cw_ref, poolw_ref, pools_ref,
                    q_out, k_out, v_out, pm_out, z_out, g_out, gt_out, il_s, *, tile, seq, heads):
    j = pl.program_id(1)
    nj = pl.num_programs(1)
    n = tile + 2 * HALO
    dn = heads * HEAD_DIM
    xe = jnp.concatenate([xp_ref[0], xm_ref[0], xn_ref[0]], axis=0)
    h = _rms(xe, n1w_ref[...])
    h = jnp.where(_halo_valid(n, tile, j, nj), h, 0.0)
    hm = h[HALO:HALO + tile].astype(BF16)
    hi = _interleave(il_s, h).astype(BF16)

    def conv_silu(w_ref, col0):
        t = _dot(hi, w_ref[...])
        acc = None
        for tt in range(SHORT_CONV):
            term = cw_ref[tt:tt + 1, col0:col0 + dn] * _shift_interleaved(t, tt - SHORT_CONV // 2)
            acc = term if acc is None else acc + term
        return acc * jax.nn.sigmoid(acc)

    def l2n(y, scale):
        outs = []
        for hh in range(heads):
            th = y[:, hh * HEAD_DIM:(hh + 1) * HEAD_DIM]
            ss = jnp.sum(th * th, axis=-1, keepdims=True)
            outs.append(th * (lax.rsqrt(ss + L2_EPS) * scale))
        return jnp.concatenate(outs, axis=-1)

    def natural(y):
        return _deinterleave(il_s, y, HALO, HALO + tile).astype(BF16)

    q_out[0] = natural(l2n(conv_silu(wq_ref, 0), HEAD_DIM ** -0.5))
    k_out[0] = natural(l2n(conv_silu(wk_ref, dn), 1.0))
    v_out[0] = natural(conv_silu(wv_ref, 2 * dn))

    pe = _dot(hi, wp_ref[...])
    slot = lax.broadcasted_iota(jnp.int32, (n, 1), 0)
    pos = j * tile - HALO + (slot // SUBLANES) + (n // SUBLANES) * (slot % SUBLANES)
    pms = []
    for g, w in enumerate(POOL_WINDOWS):
        zg = pe[:, g * POOL_GROUP_DIM:(g + 1) * POOL_GROUP_DIM]
        f = zg
        span = 1
        while span < w:
            f = f + _shift_interleaved(f, span)
            span *= 2
        ws = _shift_interleaved(f, -(w // 2))
        start = jnp.maximum(pos - w // 2, 0)
        end = jnp.minimum(pos + w - w // 2, seq)
        cnt = jnp.maximum(end - start, 1).astype(F32)
        pooled = ws / cnt - zg
        pmg = _dot(pooled.astype(BF16), poolw_ref[g])
        pms.append(pmg * pools_ref[:, g * POOL_GROUP_DIM:(g + 1) * POOL_GROUP_DIM])
    pm_out[0] = natural(jnp.concatenate(pms, axis=1))

    z_out[0] = _dot(hm, wz_ref[...]).astype(BF16)
    g_out[0] = _dot(hm, wg_ref[...]).astype(BF16)
    gt = _dot(hm, wgt_ref[...])
    gt_t = jnp.concatenate([gt[r:r + LANES].T for r in range(0, tile, LANES)], axis=1)
    gt_out[0] = gt_t[:gt_out.shape[1]]


def _in_proj(x, n1w, wq, wk, wv, wp, wz, wg, wgt, cw, poolw, pools, heads):
    B, S, D = x.shape
    tile = IN_TILE
    dn = heads * HEAD_DIM
    pool_dim = wp.shape[1]
    xprev, xmain, xnext = _halo_specs(tile, D, S)
    out_spec = lambda width: pl.BlockSpec((1, tile, width), lambda b, j: (b, j, 0))
    return pl.pallas_call(
        functools.partial(_in_proj_kernel, tile=tile, seq=S, heads=heads),
        grid=(B, S // tile),
        in_specs=[xprev, xmain, xnext, _const_spec(n1w.shape), _const_spec(wq.shape),
                  _const_spec(wk.shape), _const_spec(wv.shape), _const_spec(wp.shape),
                  _const_spec(wz.shape), _const_spec(wg.shape), _const_spec(wgt.shape),
                  _const_spec(cw.shape), _const_spec(poolw.shape), _const_spec(pools.shape)],
        out_specs=[out_spec(dn), out_spec(dn), out_spec(dn), out_spec(pool_dim), out_spec(dn),
                   out_spec(wg.shape[1]), pl.BlockSpec((1, 4 * heads, tile), lambda b, j: (b, 0, j))],
        out_shape=[jax.ShapeDtypeStruct((B, S, dn), BF16)] * 3
        + [jax.ShapeDtypeStruct((B, S, pool_dim), BF16), jax.ShapeDtypeStruct((B, S, dn), BF16),
           jax.ShapeDtypeStruct((B, S, wg.shape[1]), BF16),
           jax.ShapeDtypeStruct((B, 4 * heads, S), F32)],
        scratch_shapes=[pltpu.VMEM((max(D, dn) // LANES, tile + 2 * HALO, LANES), F32)],
        compiler_params=pltpu.CompilerParams(dimension_semantics=("parallel", "parallel"),
                                             vmem_limit_bytes=VMEM_LIMIT),
        name="in_proj",
    )(x, x, x, n1w, wq, wk, wv, wp, wz, wg, wgt, cw, poolw, pools)


def _softplus(t):
    return jnp.maximum(t, 0.0) + jnp.log1p(jnp.exp(-jnp.abs(t)))


def _scan_sum(a, axis, reverse):
    n = a.shape[axis]
    idx = lax.broadcasted_iota(jnp.int32, a.shape, axis)
    s = 1
    while s < n:
        if reverse:
            a = a + jnp.where(idx < n - s, pltpu.roll(a, n - s, axis=axis), 0.0)
        else:
            a = a + jnp.where(idx >= s, pltpu.roll(a, s, axis=axis), 0.0)
        s *= 2
    return a


def _bmm(a, b):
    return jnp.einsum("uik,ukj->uij", a, b, preferred_element_type=F32)


def _bmm_nt(a, b):
    return jnp.einsum("uik,ujk->uij", a, b, preferred_element_type=F32)


def _delta_kernel(alog_ref, dtb_ref, q_ref, k_ref, v_ref, z_ref, gates_ref, nwc_ref, o_ref,
                  gtmp_s, grow_s, brow_s, rhs1_a, rhs2_a, ut_a, gl_a, rhs1_b, rhs2_b, ut_b, gl_b, o_s,
                  *, nchunks, batch):
    C = CHUNK
    N = nchunks
    U = batch
    D = HEAD_DIM
    stages = N // U
    heads_here = q_ref.shape[-1] // D

    for hq in range(heads_here):
        hh = pl.program_id(1) * heads_here + hq
        mine = lax.broadcasted_iota(jnp.int32, (gates_ref.shape[2], C), 0) == hh
        for t in range(4):
            for n in range(N):
                every_head = gates_ref[0, t, :, pl.ds(n * C, C)]
                gtmp_s[t, pl.ds(n, 1), :] = jnp.sum(jnp.where(mine, every_head, 0.0), axis=0, keepdims=True)
        for d in range(2):
            a = jnp.exp(jnp.full((1, 1), alog_ref[d, hh], F32))
            g_r = -a * _softplus(gtmp_s[2 + d] + dtb_ref[d, hh])
            grow_s[hq, d] = _scan_sum(g_r, 1, d == 1).reshape(N, 1, C)
            brow_s[hq, d] = jax.nn.sigmoid(gtmp_s[d]).reshape(N, 1, C)

    rr = lax.broadcasted_iota(jnp.int32, (U, C, C), 1)
    cc = lax.broadcasted_iota(jnp.int32, (U, C, C), 2)

    def prep(hq, g, bufs):
        rhs1_s, rhs2_s, ut_s, gl_s = bufs
        cols = pl.ds(hq * D, D)
        for d in range(2):
            n0 = g * U if d == 0 else N - (g + 1) * U
            r0 = n0 * C
            k = k_ref[0, pl.ds(r0, U * C), cols].reshape(U, C, D)
            q = q_ref[0, pl.ds(r0, U * C), cols].reshape(U, C, D)
            kf = k.astype(F32)
            kT = jnp.swapaxes(kf, 1, 2)
            qT = jnp.swapaxes(q.astype(F32), 1, 2)
            vT = jnp.swapaxes(v_ref[0, pl.ds(r0, U * C), cols].reshape(U, C, D).astype(F32), 1, 2)
            kkq = _bmm_nt(k, jnp.concatenate([k, q], axis=1))
            KK = kkq[:, :, :C]
            QKt = kkq[:, :, C:]
            GR = grow_s[hq, d, pl.ds(n0, U)]
            BR = brow_s[hq, d, pl.ds(n0, U)]
            GC = jnp.swapaxes(jnp.broadcast_to(GR, (U, C, C)), 1, 2)
            incl = (cc >= rr) if d == 0 else (cc <= rr)
            strict = (cc > rr) if d == 0 else (cc < rr)
            decay = jnp.where(incl, jnp.exp(jnp.minimum(GR - GC, 0.0)), 0.0)
            Lt = jnp.where(strict, KK * BR * decay, 0.0)
            attn = QKt * decay
            Y = -Lt
            Lb = Lt.astype(BF16)
            P = _bmm(Lb, Lb)
            lo, hi = 0, C
            rounds = C.bit_length() - 2
            for r in range(rounds):
                m = 2 << r
                if m >= C // 4:
                    new_lo, new_hi = (0, C - m) if d == 0 else (m, C)
                    P = P[:, new_lo - lo:new_hi - lo]
                    lo, hi = new_lo, new_hi
                Pb = P.astype(BF16)
                Yb = Y.astype(BF16)
                if r < rounds - 1:
                    full = [jnp.zeros((U, lo, C), BF16), Pb, jnp.zeros((U, C - hi, C), BF16)]
                    p_full = jnp.concatenate([x for x in full if x.shape[1]], axis=1)
                    py = _bmm(Pb, jnp.concatenate([Yb, p_full], axis=2))
                    upd = Y[:, lo:hi] + P + py[:, :, :C]
                    P = py[:, :, C:]
                else:
                    upd = Y[:, lo:hi] + P + _bmm(Pb, Yb)
                parts = [Y[:, :lo], upd, Y[:, hi:]]
                Y = jnp.concatenate([x for x in parts if x.shape[1]], axis=1)
            eG = jnp.exp(GR)
            X = jnp.concatenate([kT * (BR * eG), vT * BR], axis=1)
            wu = X + _bmm(X.astype(BF16), Y.astype(BF16))
            g_last = GC[:, C - 1:C, :] if d == 0 else GC[:, 0:1, :]
            rest = jnp.exp(g_last - GR)
            kd = kf * jnp.swapaxes(jnp.broadcast_to(rest, (U, C, C)), 1, 2)
            rhs1_s[d] = jnp.concatenate([wu[:, :D], qT * eG], axis=2).astype(BF16)
            rhs2_s[d] = jnp.concatenate([attn, kd], axis=2).astype(BF16)
            ut_s[d] = wu[:, D:]
            gl_s[d] = jnp.exp(g_last)

    def finish(hq, n, ot):
        ms = jnp.mean(ot * ot, axis=0, keepdims=True)
        o = (ot * lax.rsqrt(ms + NORM_EPS) * nwc_ref[...]).T
        zz = z_ref[0, pl.ds(n * C, C), pl.ds(hq * D, D)].astype(F32)
        o_ref[0, pl.ds(n * C, C), pl.ds(hq * D, D)] = (o * (zz * jax.nn.sigmoid(zz))).astype(BF16)

    def serial(hq, g, bufs, states):
        rhs1_s, rhs2_s, ut_s, gl_s = bufs
        for j in range(U):
            new_states = []
            for d in range(2):
                jd = j if d == 0 else U - 1 - j
                step = g * U + j
                n = step if d == 0 else N - 1 - step
                ST = states[d]
                p1 = _dot(ST.astype(BF16), rhs1_s[d, jd])
                vn = ut_s[d, jd] - p1[:, :C]
                p2 = _dot(vn.astype(BF16), rhs2_s[d, jd])
                ot = p1[:, C:] + p2[:, :C]
                if step < N // 2:
                    o_s[n] = ot
                else:
                    finish(hq, n, ot + o_s[n])
                new_states.append(ST * gl_s[d, jd] + p2[:, C:])
            states = tuple(new_states)
        return states

    bufs = ((rhs1_a, rhs2_a, ut_a, gl_a), (rhs1_b, rhs2_b, ut_b, gl_b))
    zero = jnp.zeros((D, D), F32)
    work = [(hq, g) for hq in range(heads_here) for g in range(stages)]
    prep(*work[0], bufs[0])
    for i, (hq, g) in enumerate(work):
        if i + 1 < len(work):
            prep(*work[i + 1], bufs[(i + 1) % 2])
        states = serial(hq, g, bufs[i % 2], (zero, zero) if g == 0 else states)


def _delta(q, k, v, z, gates, a_log, dt_bias, nw_cols):
    B, S, dn = q.shape
    heads = dn // HEAD_DIM
    N = S // CHUNK
    U = DELTA_BATCH
    assert N % U == 0 and N % 2 == 0 and U % 2 == 0
    hp = DELTA_HEADS
    assert heads % hp == 0
    head_spec = pl.BlockSpec((1, S, hp * HEAD_DIM), lambda b, h: (b, 0, h))
    smem = pl.BlockSpec(memory_space=pltpu.SMEM)
    return pl.pallas_call(
        functools.partial(_delta_kernel, nchunks=N, batch=DELTA_BATCH),
        grid=(B, heads // hp),
        in_specs=[smem, smem, head_spec, head_spec, head_spec, head_spec,
                  pl.BlockSpec((1, 4, heads, S), lambda b, h: (b, 0, 0, 0)),
                  pl.BlockSpec((HEAD_DIM, LANES), lambda b, h: (0, 0))],
        out_specs=head_spec,
        out_shape=jax.ShapeDtypeStruct((B, S, dn), BF16),
        scratch_shapes=[
            pltpu.VMEM((4, N, CHUNK), F32),
            pltpu.VMEM((hp, 2, N, 1, CHUNK), F32),
            pltpu.VMEM((hp, 2, N, 1, CHUNK), F32),
        ] + 2 * [
            pltpu.VMEM((2, U, HEAD_DIM, 2 * CHUNK), BF16),
            pltpu.VMEM((2, U, CHUNK, CHUNK + HEAD_DIM), BF16),
            pltpu.VMEM((2, U, HEAD_DIM, CHUNK), F32),
            pltpu.VMEM((2, U, 1, LANES), F32),
        ] + [pltpu.VMEM((N, HEAD_DIM, CHUNK), F32)],
        compiler_params=pltpu.CompilerParams(dimension_semantics=("parallel", "parallel"),
                                             vmem_limit_bytes=VMEM_LIMIT),
        name="delta_rule",
    )(a_log, dt_bias, q, k, v, z, gates, nw_cols)


def _merge_kernel(o_ref, pm_ref, g_ref, x_ref, dn_ref, po_ref, wo_ref, out_ref):
    d = out_ref.shape[-1]
    y_dn = _dot(o_ref[0], dn_ref[...])
    y_pool = _dot(pm_ref[0], po_ref[...])
    g = g_ref[0].astype(F32)
    merged = jax.nn.sigmoid(g[:, :d]) * y_pool + jax.nn.sigmoid(g[:, d:]) * y_dn
    out_ref[0] = x_ref[0] + _dot(merged.astype(BF16), wo_ref[...])


def _merge(o, pm, g, x, dn_out, pool_out, w_o):
    B, S, D = x.shape
    tile = MERGE_TILE
    spec = lambda width: pl.BlockSpec((1, tile, width), lambda b, j: (b, j, 0))
    return pl.pallas_call(
        _merge_kernel,
        grid=(B, S // tile),
        in_specs=[spec(o.shape[-1]), spec(pm.shape[-1]), spec(g.shape[-1]), spec(D),
                  _const_spec(dn_out.shape), _const_spec(pool_out.shape), _const_spec(w_o.shape)],
        out_specs=spec(D),
        out_shape=jax.ShapeDtypeStruct((B, S, D), F32),
        compiler_params=pltpu.CompilerParams(dimension_semantics=("parallel", "parallel"),
                                             vmem_limit_bytes=VMEM_LIMIT),
        name="merge",
    )(o, pm, g, x, dn_out, pool_out, w_o)


def _ffn_kernel(xp_ref, xm_ref, xn_ref, n2w_ref, up_ref, cw_ref, cb_ref, down_ref, fw_ref, out_ref,
                act_s, il_s, *, tile):
    j = pl.program_id(1)
    nj = pl.num_programs(1)
    n = tile + 2 * HALO
    hidden = down_ref.shape[0]
    xm = xm_ref[0]
    xe = jnp.concatenate([xp_ref[0], xm, xn_ref[0]], axis=0)
    h = jnp.where(_halo_valid(n, tile, j, nj), _rms(xe, n2w_ref[...]), 0.0)
    hb = _interleave(il_s, h).astype(BF16)

    def conv_up(col):
        cols = slice(col, col + FFN_CHUNK)
        u = _dot(hb, up_ref[:, cols])
        acc = None
        for tt in range(FFN_CONV):
            term = cw_ref[tt:tt + 1, cols] * _shift_interleaved(u, tt - FFN_CONV // 2)
            acc = term if acc is None else acc + term
        return acc + cb_ref[:, cols]

    for col in range(0, hidden, FFN_CHUNK):
        gate = conv_up(col)
        val = conv_up(hidden + col)
        act_s[:, col:col + FFN_CHUNK] = (gate * jax.nn.sigmoid(gate) * val).astype(BF16)

    y = _deinterleave(il_s, _dot(act_s[...], down_ref[...]), HALO, HALO + tile)
    out_ref[0] = _rms(xm + y, fw_ref[...])


def _ffn(x, n2w, up, cw, cb, down, fw):
    B, S, D = x.shape
    tile = FFN_TILE
    hidden = down.shape[0]
    xprev, xmain, xnext = _halo_specs(tile, D, S)
    return pl.pallas_call(
        functools.partial(_ffn_kernel, tile=tile),
        grid=(B, S // tile),
        in_specs=[xprev, xmain, xnext, _const_spec(n2w.shape), _const_spec(up.shape),
                  _const_spec(cw.shape), _const_spec(cb.shape), _const_spec(down.shape),
                  _const_spec(fw.shape)],
        out_specs=pl.BlockSpec((1, tile, D), lambda b, j: (b, j, 0)),
        out_shape=jax.ShapeDtypeStruct((B, S, D), F32),
        scratch_shapes=[pltpu.VMEM((tile + 2 * HALO, hidden), BF16),
                        pltpu.VMEM((D // LANES, tile + 2 * HALO, LANES), F32)],
        compiler_params=pltpu.CompilerParams(dimension_semantics=("parallel", "parallel"),
                                             vmem_limit_bytes=VMEM_LIMIT),
        name="ffn",
    )(x, x, x, n2w, up, cw, cb, down, fw)


def kernel(x, norm1_w, w_in, pool_w, pool_scale, pool_out, qkv_conv_w, a_log, dt_bias, dn_norm_w,
           dn_out, w_o, norm2_w, ffn_up, ffn_conv_w, ffn_conv_b, ffn_down, final_norm_w):
    B, S, D = x.shape
    depth = w_in.shape[0]
    pool_dim = pool_out.shape[1]
    dn = dn_out.shape[1]
    heads = a_log.shape[-1]
    F = ffn_down.shape[1]
    assert depth == 1, "the final rmsnorm is fused into the (single) layer's channel mixer"
    assert dn == heads * HEAD_DIM and CHUNK == HEAD_DIM == LANES
    assert S % (CHUNK * DELTA_BATCH) == 0 and S // CHUNK <= LANES
    assert S % IN_TILE == 0 and S % MERGE_TILE == 0 and S % FFN_TILE == 0 and F % FFN_CHUNK == 0
    N = S // CHUNK
    row = lambda t: t.reshape(1, -1).astype(F32)

    for l in range(depth):
        wl = w_in[l].astype(BF16)
        c0 = pool_dim
        wp = wl[:, :c0]
        wq, wk, wv, wz = (wl[:, c0 + i * dn:c0 + (i + 1) * dn] for i in range(4))
        c1 = c0 + 4 * dn
        wgt = jnp.pad(wl[:, c1:c1 + 4 * heads], ((0, 0), (0, LANES - 4 * heads)))
        wg = wl[:, c1 + 4 * heads:]

        q, k, v, pm, z, g, gates = _in_proj(
            x, row(norm1_w[l]), wq, wk, wv, wp, wz, wg, wgt, qkv_conv_w[l].astype(F32),
            pool_w[l].astype(BF16), row(pool_scale[l]), heads)

        nw_cols = jnp.broadcast_to(dn_norm_w[l].astype(F32)[:, None], (HEAD_DIM, LANES))
        o = _delta(q, k, v, z, gates.reshape(B, 4, heads, S), a_log[l].astype(F32), dt_bias[l].astype(F32),
                   nw_cols)

        x = _merge(o, pm, g, x, dn_out[l].astype(BF16), pool_out[l].astype(BF16), w_o[l].astype(BF16))

        x = _ffn(x, row(norm2_w[l]), ffn_up[l].astype(BF16), ffn_conv_w[l].astype(F32),
                 row(ffn_conv_b[l]), ffn_down[l].astype(BF16), row(final_norm_w))
    return x
```
